```python
import jax, jax.numpy as jnp
from jax import lax
import numpy as np

D_MODEL = 1024
BATCH = 8
SEQ = 2048
DEPTH = 2

CHUNK = 64
Q_BLOCK = 128
ROPE_THETA = 10000.0
LN_EPS = 1e-5
RMS_EPS = 1e-6
NEG = -1e30

A_HEADS = 8
A_NOPE = 64
A_ROPE = 32
A_VDIM = 64
A_Q_RANK = 384
A_KV_RANK = 256

B_HEADS = 8
B_KV_HEADS = 2
B_HDIM = 64
IDX_HEADS = 4
IDX_DIM = 64
TOPK_MAX = 256

C_HEADS = 8
C_KV_HEADS = 2
C_HDIM = 64
WINDOW = 128
WIN_CHUNKS = WINDOW // CHUNK

N_BRANCH = 3
BRANCH_WIDTH = 512

N_EXPERTS = 16
N_GROUPS = 4
EXPERTS_PER_GROUP = N_EXPERTS // N_GROUPS
TOP_K = 2
D_EXPERT = 512

ALPHA = (2.0 * DEPTH) ** 0.25
BETA = (8.0 * DEPTH) ** -0.25

SPLIT_SIZES = (
    A_Q_RANK, A_KV_RANK, A_ROPE,
    B_HEADS * B_HDIM, B_KV_HEADS * B_HDIM, B_KV_HEADS * B_HDIM,
    IDX_HEADS * IDX_DIM, IDX_DIM, IDX_HEADS,
    C_HEADS * C_HDIM, C_KV_HEADS * C_HDIM, C_KV_HEADS * C_HDIM,
    N_BRANCH * D_MODEL,
)
IN_COLS = sum(SPLIT_SIZES)

kernel_name = "hybrid_mla_dsa_swa_grouped_moe_deepnorm"


def layer_norm(x, g, b):
    xf = x.astype(jnp.float32)
    mu = jnp.mean(xf, axis=-1, keepdims=True)
    var = jnp.mean(jnp.square(xf - mu), axis=-1, keepdims=True)
    return ((xf - mu) * lax.rsqrt(var + LN_EPS) * g + b).astype(x.dtype)


def rms_norm(x, g):
    xf = x.astype(jnp.float32)
    return (xf * lax.rsqrt(jnp.mean(jnp.square(xf), axis=-1, keepdims=True) + RMS_EPS) * g).astype(x.dtype)


def rope(x, pos):
    half = x.shape[-1] // 2
    inv = ROPE_THETA ** (-jnp.arange(half, dtype=jnp.float32) / half)
    ang = pos.astype(jnp.float32)[:, None] * inv[None, :]
    cos = jnp.cos(ang)[:, None, :].astype(x.dtype)
    sin = jnp.sin(ang)[:, None, :].astype(x.dtype)
    x1, x2 = x[..., :half], x[..., half:]
    return jnp.concatenate([x1 * cos - x2 * sin, x2 * cos + x1 * sin], axis=-1)


def to_blocks(a):
    b, t = a.shape[:2]
    return jnp.moveaxis(a.reshape((b, t // Q_BLOCK, Q_BLOCK) + a.shape[2:]), 1, 0)


def from_blocks(a):
    nb, b, qb = a.shape[:3]
    return jnp.moveaxis(a, 0, 1).reshape((b, nb * qb) + a.shape[3:])


def mla_branch(q_lat, kv_lat, k_pe, q_ln_g, kv_ln_g, w_uq, w_ukv, pos):
    B, T, _ = q_lat.shape
    q = (rms_norm(q_lat, q_ln_g) @ w_uq).reshape(B, T, A_HEADS, A_NOPE + A_ROPE)
    q_nope, q_pe = q[..., :A_NOPE], rope(q[..., A_NOPE:], pos)
    kv = (rms_norm(kv_lat, kv_ln_g) @ w_ukv).reshape(B, T, A_HEADS, A_NOPE + A_VDIM)
    k_nope, v = kv[..., :A_NOPE], kv[..., A_NOPE:]
    k_pe = rope(k_pe[:, :, None, :], pos)[:, :, 0, :]
    scale = (A_NOPE + A_ROPE) ** -0.5
    key_chunk = jnp.arange(T) // CHUNK

    def block(args):
        qn, qp, qpos = args
        s = (jnp.einsum('bqhd,bshd->bhqs', qn, k_nope)
             + jnp.einsum('bqhd,bsd->bhqs', qp, k_pe)).astype(jnp.float32) * scale
        mask = key_chunk[None, :] <= (qpos // CHUNK)[:, None]
        p = jax.nn.softmax(jnp.where(mask, s, NEG), axis=-1).astype(v.dtype)
        return jnp.einsum('bhqs,bshd->bqhd', p, v)

    out = lax.map(block, (to_blocks(q_nope), to_blocks(q_pe), pos.reshape(-1, Q_BLOCK)))
    return from_blocks(out).reshape(B, T, A_HEADS * A_VDIM)


def dsa_branch(q, k, v, q_idx, k_idx, w_idx, pos):
    B, T, _ = q.shape
    rep = B_HEADS // B_KV_HEADS
    q = rope(q.reshape(B, T, B_HEADS, B_HDIM), pos)
    k = rope(k.reshape(B, T, B_KV_HEADS, B_HDIM), pos)
    v = v.reshape(B, T, B_KV_HEADS, B_HDIM)
    q_idx = rope(q_idx.reshape(B, T, IDX_HEADS, IDX_DIM), pos)
    k_idx = rope(k_idx[:, :, None, :], pos)[:, :, 0, :]
    w_idx = w_idx * (IDX_HEADS * IDX_DIM) ** -0.5
    n_sel = min(TOPK_MAX, T // 4)
    scale = B_HDIM ** -0.5
    key_chunk = jnp.arange(T) // CHUNK
    gather = jax.vmap(lambda a, i: a[i])

    def block(args):
        qb, qib, wb, qpos = args
        qchunk = qpos // CHUNK
        rel = jax.nn.relu(jnp.einsum('bqhd,bsd->bqhs', qib, k_idx).astype(jnp.float32))
        score = jnp.einsum('bqh,bqhs->bqs', wb.astype(jnp.float32), rel)
        score = jnp.where(key_chunk[None, None, :] <= qchunk[None, :, None], score, NEG)
        _, idx = lax.top_k(score, n_sel)
        ks, vs = gather(k, idx), gather(v, idx)
        valid = (idx // CHUNK) <= qchunk[None, :, None]
        qg = qb.reshape(B, Q_BLOCK, B_KV_HEADS, rep, B_HDIM)
        s = jnp.einsum('bqgrd,bqkgd->bgrqk', qg, ks).astype(jnp.float32) * scale
        s = jnp.where(valid[:, None, None], s, NEG)
        p = jax.nn.softmax(s, axis=-1).astype(vs.dtype)
        o = jnp.einsum('bgrqk,bqkgd->bqgrd', p, vs)
        return o.reshape(B, Q_BLOCK, B_HEADS * B_HDIM)

    out = lax.map(block, (to_blocks(q), to_blocks(q_idx), to_blocks(w_idx), pos.reshape(-1, Q_BLOCK)))
    return from_blocks(out)


def swa_branch(q, k, v, sinks, pos):
    B, T, _ = q.shape
    nC = T // CHUNK
    rep = C_HEADS // C_KV_HEADS
    q = rope(q.reshape(B, T, C_HEADS, C_HDIM), pos).reshape(B, nC, CHUNK, C_KV_HEADS, rep, C_HDIM)
    k = rope(k.reshape(B, T, C_KV_HEADS, C_HDIM), pos)
    v = v.reshape(B, T, C_KV_HEADS, C_HDIM)
    pad = WIN_CHUNKS * CHUNK
    band_len = (WIN_CHUNKS + 1) * CHUNK

    def band(a):
        ap = jnp.pad(a, ((0, 0), (pad, 0), (0, 0), (0, 0)))
        ap = ap.reshape(B, nC + WIN_CHUNKS, CHUNK, C_KV_HEADS, C_HDIM)
        return jnp.concatenate([ap[:, j:j + nC] for j in range(WIN_CHUNKS + 1)], axis=2)

    kb, vb = band(k), band(v)
    key_pos = jnp.arange(nC)[:, None] * CHUNK - pad + jnp.arange(band_len)[None, :]
    valid = (key_pos >= 0)[:, None, :]
    s = jnp.einsum('bcqgrd,bckgd->bgrcqk', q, kb).astype(jnp.float32) * C_HDIM ** -0.5
    s = jnp.where(valid, s, NEG)
    sink = jnp.broadcast_to(sinks.astype(jnp.float32).reshape(C_KV_HEADS, rep, 1, 1, 1), s.shape[:-1] + (1,))
    p = jax.nn.softmax(jnp.concatenate([s, sink], axis=-1), axis=-1)[..., :-1].astype(vb.dtype)
    o = jnp.einsum('bgrcqk,bckgd->bcqgrd', p, vb)
    return o.reshape(B, T, C_HEADS * C_HDIM)


def mixer(h, w_in, a_q_ln_g, a_kv_ln_g, a_w_uq, a_w_ukv, c_sinks, w_br_a, w_br_b, w_br_c, w_out, pos):
    B, T, D = h.shape
    points = [int(p) for p in np.cumsum(SPLIT_SIZES)[:-1]]
    (a_q, a_kv, a_kpe, b_q, b_k, b_v, b_qi, b_ki, b_wi,
     c_q, c_k, c_v, gates) = jnp.split(h @ w_in, points, axis=-1)
    y_a = mla_branch(a_q, a_kv, a_kpe, a_q_ln_g, a_kv_ln_g, a_w_uq, a_w_ukv, pos) @ w_br_a
    y_b = dsa_branch(b_q, b_k, b_v, b_qi, b_ki, b_wi, pos) @ w_br_b
    y_c = swa_branch(c_q, c_k, c_v, c_sinks, pos) @ w_br_c
    g = jax.nn.sigmoid(gates).reshape(B, T, N_BRANCH, D)
    merged = g[:, :, 0] * y_a + g[:, :, 1] * y_b + g[:, :, 2] * y_c
    return merged @ w_out


def moe(h, w_router, router_bias, w_gate, w_up, w_down):
    B, T, D = h.shape
    hf = h.reshape(B * T, D)
    scores = jax.nn.sigmoid((hf @ w_router).astype(jnp.float32))
    biased = scores + router_bias.astype(jnp.float32)
    grp = biased.reshape(-1, N_GROUPS, EXPERTS_PER_GROUP)
    group_score = lax.top_k(grp, TOP_K)[0].sum(-1)
    g_sel = jnp.argmax(group_score, axis=-1)
    in_group = (jnp.arange(N_EXPERTS) // EXPERTS_PER_GROUP)[None, :] == g_sel[:, None]
    _, top_idx = lax.top_k(jnp.where(in_group, biased, NEG), TOP_K)
    w = jnp.take_along_axis(scores, top_idx, axis=-1)
    w = w / jnp.sum(w, axis=-1, keepdims=True)
    gate = jnp.sum(jax.nn.one_hot(top_idx, N_EXPERTS, dtype=jnp.float32) * w[..., None], axis=1)
    act = jax.nn.silu(jnp.einsum('nd,edf->nef', hf, w_gate)) * jnp.einsum('nd,edf->nef', hf, w_up)
    act = act * gate.astype(act.dtype)[:, :, None]
    return jnp.einsum('nef,efd->nd', act, w_down).reshape(B, T, D)


def setup_inputs(seed: int = 0) -> dict:
    key = jax.random.key(seed)
    ks = jax.random.split(key, 24)
    f32 = jnp.float32
    L, D = DEPTH, D_MODEL

    def nrm(k, shape, fan_in, scale=1.0):
        return jax.random.normal(k, shape, f32) * (scale * fan_in ** -0.5)

    def gain(k, shape):
        return 1.0 + 0.05 * jax.random.normal(k, shape, f32)

    return {
        "x": jax.random.normal(ks[0], (BATCH, SEQ, D), f32),
        "ln_in_g": gain(ks[1], (D,)),
        "ln_in_b": 0.02 * jax.random.normal(ks[2], (D,), f32),
        "w_in": nrm(ks[3], (L, D, IN_COLS), D),
        "a_q_ln_g": gain(ks[4], (L, A_Q_RANK)),
        "a_kv_ln_g": gain(ks[5], (L, A_KV_RANK)),
        "a_w_uq": nrm(ks[6], (L, A_Q_RANK, A_HEADS * (A_NOPE + A_ROPE)), A_Q_RANK),
        "a_w_ukv": nrm(ks[7], (L, A_KV_RANK, A_HEADS * (A_NOPE + A_VDIM)), A_KV_RANK),
        "c_sinks": 0.5 * jax.random.normal(ks[8], (L, C_HEADS), f32),
        "w_br_a": nrm(ks[9], (L, A_HEADS * A_VDIM, D), A_HEADS * A_VDIM),
        "w_br_b": nrm(ks[10], (L, B_HEADS * B_HDIM, D), B_HEADS * B_HDIM),
        "w_br_c": nrm(ks[11], (L, C_HEADS * C_HDIM, D), C_HEADS * C_HDIM),
        "w_out": nrm(ks[12], (L, D, D), D, BETA),
        "ln1_g": gain(ks[13], (L, D)),
        "ln1_b": 0.02 * jax.random.normal(ks[14], (L, D), f32),
        "w_router": nrm(ks[15], (D, N_EXPERTS), D),
        "router_bias": 0.01 * jax.random.normal(ks[16], (N_EXPERTS,), f32),
        "w_exp_gate": nrm(ks[17], (L, N_EXPERTS, D, D_EXPERT), D),
        "w_exp_up": nrm(ks[18], (L, N_EXPERTS, D, D_EXPERT), D),
        "w_exp_down": nrm(ks[19], (L, N_EXPERTS, D_EXPERT, D), D_EXPERT, BETA),
        "ln2_g": gain(ks[20], (L, D)),
        "ln2_b": 0.02 * jax.random.normal(ks[21], (L, D), f32),
    }


def reference(x, ln_in_g, ln_in_b, w_in, a_q_ln_g, a_kv_ln_g, a_w_uq, a_w_ukv, c_sinks,
              w_br_a, w_br_b, w_br_c, w_out, ln1_g, ln1_b, w_router, router_bias,
              w_exp_gate, w_exp_up, w_exp_down, ln2_g, ln2_b):
    pos = jnp.arange(x.shape[1], dtype=jnp.int32)
    x = layer_norm(x, ln_in_g, ln_in_b)
    for l in range(DEPTH):
        mix = mixer(x, w_in[l], a_q_ln_g[l], a_kv_ln_g[l], a_w_uq[l], a_w_ukv[l], c_sinks[l],
                    w_br_a[l], w_br_b[l], w_br_c[l], w_out[l], pos)
        x = layer_norm(ALPHA * x + mix, ln1_g[l], ln1_b[l])
        ffn = moe(x, w_router, router_bias, w_exp_gate[l], w_exp_up[l], w_exp_down[l])
        x = layer_norm(ALPHA * x + ffn, ln2_g[l], ln2_b[l])
    return x
```

```python
import functools

import jax
import jax.numpy as jnp
import numpy as np
from jax import lax
from jax.experimental import pallas as pl
from jax.experimental.pallas import tpu as pltpu

F32 = jnp.float32
BF16 = jnp.bfloat16
I32 = jnp.int32

D_MODEL = 1024
DEPTH = 2
CHUNK = 64
ROPE_THETA = 10000.0
LN_EPS = 1e-5
RMS_EPS = 1e-6
NEG = -1e30

A_HEADS, A_NOPE, A_ROPE, A_VDIM, A_Q_RANK, A_KV_RANK = 8, 64, 32, 64, 384, 256
B_HEADS, B_KV_HEADS, B_HDIM, IDX_HEADS, IDX_DIM, TOPK_MAX = 8, 2, 64, 4, 64, 256
C_HEADS, C_KV_HEADS, C_HDIM, WINDOW = 8, 2, 64, 128
WIN_CHUNKS = WINDOW // CHUNK
N_EXPERTS, N_GROUPS, TOP_K, D_EXPERT = 16, 4, 2, 512
EXPERTS_PER_GROUP = N_EXPERTS // N_GROUPS
ALPHA = (2.0 * DEPTH) ** 0.25

SPLIT_SIZES = (
    A_Q_RANK, A_KV_RANK, A_ROPE,
    B_HEADS * B_HDIM, B_KV_HEADS * B_HDIM, B_KV_HEADS * B_HDIM,
    IDX_HEADS * IDX_DIM, IDX_DIM, IDX_HEADS,
    C_HEADS * C_HDIM, C_KV_HEADS * C_HDIM, C_KV_HEADS * C_HDIM,
    3 * D_MODEL,
)

LANES = 128
TM = 512
TQ = 256
TK = 256
TM_MOE = 1024
VMEM_LIMIT = 56 * 1024 * 1024
INT_MIN = -(2 ** 31)


def _cparams(sem):
    return pltpu.CompilerParams(dimension_semantics=sem, vmem_limit_bytes=VMEM_LIMIT)


def _dot(a, b):
    return jnp.dot(a, b, preferred_element_type=F32)


def _dot_nt(a, b):
    return lax.dot_general(a, b, (((1,), (1,)), ((), ())), preferred_element_type=F32)


def _layer_norm(z, g, b):
    mu = jnp.mean(z, axis=-1, keepdims=True)
    zc = z - mu
    var = jnp.mean(zc * zc, axis=-1, keepdims=True)
    return zc * lax.rsqrt(var + LN_EPS) * g + b


def _ln_kernel(x_ref, g_ref, b_ref, of_ref, ob_ref):
    y = _layer_norm(x_ref[...], g_ref[...], b_ref[...])
    of_ref[...] = y
    ob_ref[...] = y.astype(BF16)


def _ln_call(x, g, b):
    n, d = x.shape
    row = pl.BlockSpec((TM, d), lambda i: (i, 0))
    vec = pl.BlockSpec((1, d), lambda i: (0, 0))
    return pl.pallas_call(
        _ln_kernel,
        grid=(n // TM,),
        in_specs=[row, vec, vec],
        out_specs=[row, row],
        out_shape=[jax.ShapeDtypeStruct((n, d), F32), jax.ShapeDtypeStruct((n, d), BF16)],
        compiler_params=_cparams(("parallel",)),
        name="ln_in",
    )(x, g.reshape(1, d), b.reshape(1, d))


def _swap_halves(x, lane, half):
    return jnp.where((lane & (2 * half - 1)) < half,
                     pltpu.roll(x, LANES - half, 1), pltpu.roll(x, half, 1))


def _proj_kernel(segs, x_ref, w_ref, cos_ref, sin_ref, *out_refs):
    x = x_ref[...]
    cos = cos_ref[...]
    sin = sin_ref[...]
    lane = lax.broadcasted_iota(I32, cos.shape, 1)
    for (start, width, mode), o_ref in zip(segs, out_refs):
        y = _dot(x, w_ref[:, start:start + width])
        if mode == "rope":
            for c in range(width // LANES):
                yc = y[:, c * LANES:(c + 1) * LANES]
                r = yc * cos + _swap_halves(yc, lane, B_HDIM // 2) * sin
                o_ref[:, c * LANES:(c + 1) * LANES] = r.astype(o_ref.dtype)
        elif mode == "sigmoid":
            o_ref[...] = jax.nn.sigmoid(y).astype(o_ref.dtype)
        else:
            o_ref[...] = y.astype(o_ref.dtype)


def _proj_call(xb, w, cos, sin, segs, dtypes, name, seq):
    n, d = xb.shape
    cols = w.shape[1]
    tpb = seq // TM
    row = lambda wdt: pl.BlockSpec((TM, wdt), lambda i: (i, 0))
    tab = pl.BlockSpec((TM, LANES), lambda i: (i % tpb, 0))
    return pl.pallas_call(
        functools.partial(_proj_kernel, segs),
        grid=(n // TM,),
        in_specs=[row(d), pl.BlockSpec((d, cols), lambda i: (0, 0)), tab, tab],
        out_specs=[row(s[1]) for s in segs],
        out_shape=[jax.ShapeDtypeStruct((n, s[1]), dt) for s, dt in zip(segs, dtypes)],
        compiler_params=_cparams(("parallel",)),
        name=name,
    )(xb, w, cos, sin)


def _mla_proj_kernel(x_ref, w1_ref, qg_ref, kvg_ref, wq_ref, wkv_ref,
                     cq_ref, sq_ref, ck_ref, sk_ref, q_ref, k_ref, v_ref):
    y = _dot(x_ref[...], w1_ref[...])
    ql = y[:, :A_Q_RANK]
    kvl = y[:, A_Q_RANK:A_Q_RANK + A_KV_RANK]
    kpe = y[:, A_Q_RANK + A_KV_RANK:]
    qn = ql * lax.rsqrt(jnp.mean(ql * ql, axis=-1, keepdims=True) + RMS_EPS) * qg_ref[...]
    kvn = kvl * lax.rsqrt(jnp.mean(kvl * kvl, axis=-1, keepdims=True) + RMS_EPS) * kvg_ref[...]
    q = _dot(qn.astype(BF16), wq_ref[...])
    kv = _dot(kvn.astype(BF16), wkv_ref[...])
    lane = lax.broadcasted_iota(I32, kpe.shape, 1)
    half = A_ROPE // 2
    kpe_sw = jnp.where(lane < half, pltpu.roll(kpe, LANES - half, 1), pltpu.roll(kpe, half, 1))
    kpe_r = kpe * ck_ref[...] + kpe_sw * sk_ref[...]
    kpe_p = pltpu.roll(kpe_r, A_NOPE, 1)
    cq = cq_ref[...]
    sq = sq_ref[...]
    for h in range(A_HEADS):
        sl = slice(h * LANES, (h + 1) * LANES)
        qc = q[:, sl]
        qsw = jnp.where(lane < A_NOPE + half, pltpu.roll(qc, LANES - half, 1), pltpu.roll(qc, half, 1))
        q_ref[:, sl] = (qc * cq + qsw * sq).astype(BF16)
        k_ref[:, sl] = (kv[:, sl] + kpe_p).astype(BF16)
    v_ref[...] = kv[:, A_HEADS * LANES:].astype(BF16)


def _mla_proj_call(xb, w1, qg, kvg, wq, wkv, tabs, seq):
    n, d = xb.shape
    tpb = seq // TM
    full = lambda a: pl.BlockSpec(a.shape, lambda i: (0, 0))
    row = lambda wdt: pl.BlockSpec((TM, wdt), lambda i: (i, 0))
    tab = pl.BlockSpec((TM, LANES), lambda i: (i % tpb, 0))
    hw = A_HEADS * LANES
    return pl.pallas_call(
        _mla_proj_kernel,
        grid=(n // TM,),
        in_specs=[row(d), full(w1), full(qg), full(kvg), full(wq), full(wkv), tab, tab, tab, tab],
        out_specs=[row(hw), row(hw), row(hw)],
        out_shape=[jax.ShapeDtypeStruct((n, hw), BF16)] * 3,
        compiler_params=_cparams(("parallel",)),
        name="mla_proj",
    )(xb, w1, qg, kvg, wq, wkv, *tabs)


def _chunk_of(pos):
    return lax.shift_right_logical(pos, 6)


def _diag_visible():
    r = lax.broadcasted_iota(I32, (TQ, TK), 0)
    c = lax.broadcasted_iota(I32, (TQ, TK), 1)
    return _chunk_of(c) <= _chunk_of(r)


def _online_step(h, s, v, m_ref, l_ref, acc_ref):
    m_prev = m_ref[h]
    m_new = jnp.maximum(m_prev, jnp.max(s, axis=1, keepdims=True))
    alpha = jnp.exp(m_prev - m_new)
    p = jnp.exp(s - m_new)
    l_ref[h] = alpha * l_ref[h] + jnp.sum(p, axis=1, keepdims=True)
    acc_ref[h] = alpha * acc_ref[h] + _dot(p.astype(BF16), v)
    m_ref[h] = m_new


def _init_state(m_ref, l_ref, acc_ref):
    m_ref[...] = jnp.full(m_ref.shape, NEG, F32)
    l_ref[...] = jnp.zeros(l_ref.shape, F32)
    acc_ref[...] = jnp.zeros(acc_ref.shape, F32)


def _write_pairs(o_ref, l_ref, acc_ref, n_heads):
    for j in range(n_heads // 2):
        o = acc_ref[2 * j] / l_ref[2 * j] + acc_ref[2 * j + 1] / l_ref[2 * j + 1]
        o_ref[:, j * LANES:(j + 1) * LANES] = o.astype(o_ref.dtype)


def _mla_attn_kernel(q_ref, k_ref, v_ref, o_ref, m_ref, l_ref, acc_ref):
    i = pl.program_id(1)
    _init_state(m_ref, l_ref, acc_ref)
    diag = _diag_visible()

    def block(j, masked):
        rows = pl.ds(pl.multiple_of(j * TK, TK), TK)
        for h in range(A_HEADS):
            sl = slice(h * LANES, (h + 1) * LANES)
            s = _dot_nt(q_ref[:, sl], k_ref[rows, sl])
            if masked:
                s = jnp.where(diag, s, NEG)
            _online_step(h, s, v_ref[rows, sl], m_ref, l_ref, acc_ref)

    def body(j, c):
        block(j, False)
        return c

    lax.fori_loop(0, i, body, 0)
    block(i, True)
    _write_pairs(o_ref, l_ref, acc_ref, A_HEADS)


def _mla_attn_call(q, k, v, batch, seq):
    nq = seq // TQ
    hw = A_HEADS * LANES
    ow = A_HEADS * A_VDIM
    return pl.pallas_call(
        _mla_attn_kernel,
        grid=(batch, nq),
        in_specs=[pl.BlockSpec((TQ, hw), lambda b, i: (b * nq + i, 0)),
                  pl.BlockSpec((seq, hw), lambda b, i: (b, 0)),
                  pl.BlockSpec((seq, hw), lambda b, i: (b, 0))],
        out_specs=pl.BlockSpec((TQ, ow), lambda b, i: (b * nq + i, 0)),
        out_shape=jax.ShapeDtypeStruct((batch * seq, ow), BF16),
        scratch_shapes=[pltpu.VMEM((A_HEADS, TQ, 1), F32), pltpu.VMEM((A_HEADS, TQ, 1), F32),
                        pltpu.VMEM((A_HEADS, TQ, LANES), F32)],
        compiler_params=_cparams(("parallel", "arbitrary")),
        name="mla_attn",
    )(q, k, v)


def _dsa_attn_kernel(n_sel, q_ref, kk_ref, vv_ref, qi_ref, ki_ref, wi_ref, o_ref,
                     key_ref, bias_ref, j_ref, m_ref, l_ref, acc_ref):
    i = pl.program_id(1)
    nblk = i + 1
    diag = _diag_visible()
    lane = lax.broadcasted_iota(I32, (TQ, TK), 1)
    w = wi_ref[...]

    def score_block(jb, c):
        rows = pl.ds(pl.multiple_of(jb * TK, TK), TK)
        sc = jnp.zeros((TQ, TK), F32)
        for p in range(IDX_HEADS // 2):
            qp = qi_ref[:, p * LANES:(p + 1) * LANES]
            ra = jnp.maximum(_dot_nt(qp, ki_ref[rows, 0:LANES]), 0.0)
            rb = jnp.maximum(_dot_nt(qp, ki_ref[rows, LANES:2 * LANES]), 0.0)
            sc = sc + w[:, 2 * p:2 * p + 1] * ra + w[:, 2 * p + 1:2 * p + 2] * rb
        sc = jnp.where(jnp.logical_or(jb < i, diag), sc, NEG)
        bits = lax.bitcast_convert_type(sc + 0.0, I32)
        key_ref[jb] = jnp.where(bits < 0, bits ^ 0x7FFFFFFF, bits)
        return c

    lax.fori_loop(0, nblk, score_block, 0)

    def count(pred):
        def body(jb, acc):
            c = pred(key_ref[jb], jb * TK + lane).astype(I32)
            return acc + c[:, :LANES] + c[:, LANES:]
        acc = lax.fori_loop(0, nblk, body, jnp.zeros((TQ, LANES), I32))
        return jnp.sum(acc, axis=1, keepdims=True)

    c0 = count(lambda k, g: k >= 0)
    t0 = jnp.where(c0 >= n_sel, 0, INT_MIN).astype(I32)

    def value_step(s, t):
        cand = t | lax.shift_left(jnp.int32(1), 30 - s)
        c = count(lambda k, g: k >= cand)
        return jnp.where(c >= n_sel, cand, t)

    t = lax.fori_loop(0, 31, value_step, t0)

    c_gt = count(lambda k, g: k > t)
    c_eq = count(lambda k, g: k == t)
    need = n_sel - c_gt
    j_ref[...] = jnp.full((TQ, 1), 2 ** 30, I32)

    @pl.when(jnp.max(c_eq - need) > 0)
    def _():
        def index_step(s, jv):
            cand = jv | lax.shift_left(jnp.int32(1), 14 - s)
            c = count(lambda k, g: jnp.logical_and(k == t, g < cand))
            return jnp.where(c < need, cand, jv)
        j_ref[...] = lax.fori_loop(0, 15, index_step, jnp.zeros((TQ, 1), I32))

    jv = j_ref[...]

    def bias_block(jb, c):
        k = key_ref[jb]
        g = jb * TK + lane
        sel = jnp.logical_or(k > t, jnp.logical_and(k == t, g <= jv))
        vis = jnp.logical_or(jb < i, diag)
        bias_ref[jb] = jnp.where(jnp.logical_and(sel, vis), 0.0, NEG)
        return c

    lax.fori_loop(0, nblk, bias_block, 0)

    _init_state(m_ref, l_ref, acc_ref)
    rep = B_HEADS // B_KV_HEADS

    def attn_block(jb, c):
        rows = pl.ds(pl.multiple_of(jb * TK, TK), TK)
        bias = bias_ref[jb]
        for h in range(B_HEADS):
            kvc = 2 * (h // rep) + (h % 2)
            ksl = slice(kvc * LANES, (kvc + 1) * LANES)
            s = _dot_nt(q_ref[:, (h // 2) * LANES:(h // 2 + 1) * LANES], kk_ref[rows, ksl]) + bias
            _online_step(h, s, vv_ref[rows, ksl], m_ref, l_ref, acc_ref)
        return c

    lax.fori_loop(0, nblk, attn_block, 0)
    _write_pairs(o_ref, l_ref, acc_ref, B_HEADS)


def _dsa_attn_call(q, kk, vv, qi, ki, wi, batch, seq):
    nq = seq // TQ
    n_sel = min(TOPK_MAX, seq // 4)
    qrow = lambda wdt: pl.BlockSpec((TQ, wdt), lambda b, i: (b * nq + i, 0))
    brow = lambda wdt: pl.BlockSpec((seq, wdt), lambda b, i: (b, 0))
    return pl.pallas_call(
        functools.partial(_dsa_attn_kernel, n_sel),
        grid=(batch, nq),
        in_specs=[qrow(q.shape[1]), brow(kk.shape[1]), brow(vv.shape[1]),
                  qrow(qi.shape[1]), brow(ki.shape[1]), qrow(wi.shape[1])],
        out_specs=qrow(B_HEADS * B_HDIM),
        out_shape=jax.ShapeDtypeStruct((batch * seq, B_HEADS * B_HDIM), BF16),
        scratch_shapes=[pltpu.VMEM((nq, TQ, TK), I32), pltpu.VMEM((nq, TQ, TK), F32),
                        pltpu.VMEM((TQ, 1), I32),
                        pltpu.VMEM((B_HEADS, TQ, 1), F32), pltpu.VMEM((B_HEADS, TQ, 1), F32),
                        pltpu.VMEM((B_HEADS, TQ, LANES), F32)],
        compiler_params=_cparams(("parallel", "arbitrary")),
        name="dsa_attn",
    )(q, kk, vv, qi, ki, wi)


SWA_SLAB = TQ + WIN_CHUNKS * CHUNK


def _swa_attn_kernel(sink_ref, q_ref, kk_ref, vv_ref, o_ref):
    i = pl.program_id(1)
    s0 = pl.multiple_of(jnp.maximum(i * TQ - WIN_CHUNKS * CHUNK, 0), WIN_CHUNKS * CHUNK)
    rows = pl.ds(s0, SWA_SLAB)
    qc = _chunk_of(i * TQ + lax.broadcasted_iota(I32, (TQ, SWA_SLAB), 0))
    kc = _chunk_of(s0 + lax.broadcasted_iota(I32, (TQ, SWA_SLAB), 1))
    valid = jnp.logical_and(kc <= qc, kc >= qc - WIN_CHUNKS)
    rep = C_HEADS // C_KV_HEADS
    outs = []
    for h in range(C_HEADS):
        kvc = 2 * (h // rep) + (h % 2)
        ksl = slice(kvc * LANES, (kvc + 1) * LANES)
        s = _dot_nt(q_ref[:, (h // 2) * LANES:(h // 2 + 1) * LANES], kk_ref[rows, ksl])
        s = jnp.where(valid, s, NEG)
        sink = sink_ref[h]
        m = jnp.maximum(jnp.max(s, axis=1, keepdims=True), sink)
        p = jnp.exp(s - m)
        l = jnp.sum(p, axis=1, keepdims=True) + jnp.exp(sink - m)
        outs.append(_dot(p.astype(BF16), vv_ref[rows, ksl]) / l)
    for j in range(C_HEADS // 2):
        o_ref[:, j * LANES:(j + 1) * LANES] = (outs[2 * j] + outs[2 * j + 1]).astype(o_ref.dtype)


def _swa_attn_call(sinks, q, kk, vv, batch, seq):
    nq = seq // TQ
    qrow = lambda wdt: pl.BlockSpec((TQ, wdt), lambda b, i: (b * nq + i, 0))
    brow = lambda wdt: pl.BlockSpec((seq, wdt), lambda b, i: (b, 0))
    return pl.pallas_call(
        _swa_attn_kernel,
        grid=(batch, nq),
        in_specs=[pl.BlockSpec(memory_space=pltpu.SMEM), qrow(q.shape[1]), brow(kk.shape[1]), brow(vv.shape[1])],
        out_specs=qrow(C_HEADS * C_HDIM),
        out_shape=jax.ShapeDtypeStruct((batch * seq, C_HEADS * C_HDIM), BF16),
        compiler_params=_cparams(("parallel", "arbitrary")),
        name="swa_attn",
    )(sinks, q, kk, vv)


def _merge_kernel(oa_ref, ob_ref, oc_ref, ga_ref, gb_ref, gc_ref, x_ref, wa_ref, wb_ref, wc_ref, wo_ref,
                  lg_ref, lb_ref, of_ref, ob16_ref):
    merged = (ga_ref[...].astype(F32) * _dot(oa_ref[...], wa_ref[...])
              + gb_ref[...].astype(F32) * _dot(ob_ref[...], wb_ref[...])
              + gc_ref[...].astype(F32) * _dot(oc_ref[...], wc_ref[...]))
    mix = _dot(merged.astype(BF16), wo_ref[...])
    y = _layer_norm(ALPHA * x_ref[...] + mix, lg_ref[...], lb_ref[...])
    of_ref[...] = y
    ob16_ref[...] = y.astype(BF16)


def _merge_call(oa, ob, oc, ga, gb, gc, x, wa, wb, wc, wo, lg, lb):
    n, d = x.shape
    row = lambda wdt: pl.BlockSpec((TM, wdt), lambda i: (i, 0))
    full = lambda a: pl.BlockSpec(a.shape, lambda i: (0, 0))
    return pl.pallas_call(
        _merge_kernel,
        grid=(n // TM,),
        in_specs=[row(oa.shape[1]), row(ob.shape[1]), row(oc.shape[1]), row(d), row(d), row(d), row(d),
                  full(wa), full(wb), full(wc), full(wo), full(lg), full(lb)],
        out_specs=[row(d), row(d)],
        out_shape=[jax.ShapeDtypeStruct((n, d), F32), jax.ShapeDtypeStruct((n, d), BF16)],
        compiler_params=_cparams(("parallel",)),
        name="merge",
    )(oa, ob, oc, ga, gb, gc, x, wa, wb, wc, wo, lg, lb)


def _router_kernel(x_ref, wr_ref, rb_ref, gate_ref):
    logits = lax.dot_general(wr_ref[...], x_ref[...], (((1,), (1,)), ((), ())),
                             precision=lax.Precision.HIGHEST, preferred_element_type=F32)
    scores = jax.nn.sigmoid(logits)
    biased = scores + rb_ref[...]
    sc = [scores[e:e + 1, :] for e in range(N_EXPERTS)]
    bi = [biased[e:e + 1, :] for e in range(N_EXPERTS)]
    epg = EXPERTS_PER_GROUP
    gs = []
    for g in range(N_GROUPS):
        b = bi[g * epg:(g + 1) * epg]
        best = None
        for a in range(epg):
            for c in range(a + 1, epg):
                pair = b[a] + b[c]
                best = pair if best is None else jnp.maximum(best, pair)
        gs.append(best)
    gsel = []
    for g in range(N_GROUPS):
        ok = None
        for o in range(N_GROUPS):
            if o == g:
                continue
            cond = gs[g] > gs[o] if o < g else gs[g] >= gs[o]
            ok = cond if ok is None else jnp.logical_and(ok, cond)
        gsel.append(ok)
    chosen = []
    for e in range(N_EXPERTS):
        g = e // epg
        rank = jnp.zeros_like(bi[e], dtype=I32)
        for o in range(g * epg, (g + 1) * epg):
            if o == e:
                continue
            ahead = bi[o] >= bi[e] if o < e else bi[o] > bi[e]
            rank = rank + ahead.astype(I32)
        chosen.append(jnp.logical_and(gsel[g], rank < TOP_K))
    picked = [jnp.where(chosen[e], sc[e], 0.0) for e in range(N_EXPERTS)]
    total = picked[0]
    for e in range(1, N_EXPERTS):
        total = total + picked[e]
    for e in range(N_EXPERTS):
        gate_ref[e:e + 1, :] = picked[e] / total


def _router_call(x, wr_t, rb):
    n, d = x.shape
    return pl.pallas_call(
        _router_kernel,
        grid=(n // TM,),
        in_specs=[pl.BlockSpec((TM, d), lambda i: (i, 0)),
                  pl.BlockSpec((N_EXPERTS, d), lambda i: (0, 0)),
                  pl.BlockSpec((N_EXPERTS, 1), lambda i: (0, 0))],
        out_specs=pl.BlockSpec((N_EXPERTS, TM), lambda i: (0, i)),
        out_shape=jax.ShapeDtypeStruct((N_EXPERTS, n), F32),
        compiler_params=_cparams(("parallel",)),
        name="router",
    )(x, wr_t, rb)


def _moe_kernel(xb_ref, xf_ref, gate_ref, wg_ref, wu_ref, wd_ref, lg_ref, lb_ref,
                of_ref, ob_ref, acc_ref):
    e = pl.program_id(1)

    @pl.when(e == 0)
    def _():
        acc_ref[...] = jnp.zeros(acc_ref.shape, F32)

    x = xb_ref[...]
    a = _dot(x, wg_ref[0])
    u = _dot(x, wu_ref[0])
    gate = gate_ref[...]
    lane = lax.broadcasted_iota(I32, gate.shape, 1)
    gcol = jnp.sum(jnp.where(lane == e, gate, 0.0), axis=1, keepdims=True)
    h = a * jax.nn.sigmoid(a) * u * gcol
    acc_ref[...] += _dot(h.astype(BF16), wd_ref[0])

    @pl.when(e == N_EXPERTS - 1)
    def _():
        y = _layer_norm(ALPHA * xf_ref[...] + acc_ref[...], lg_ref[...], lb_ref[...])
        of_ref[...] = y
        ob_ref[...] = y.astype(BF16)


def _moe_call(xb, xf, gate, wg, wu, wd, lg, lb):
    n, d = xf.shape
    tm = min(TM_MOE, n)
    row = lambda wdt: pl.BlockSpec((tm, wdt), lambda i, e: (i, 0))
    vec = pl.BlockSpec((1, d), lambda i, e: (0, 0))
    return pl.pallas_call(
        _moe_kernel,
        grid=(n // tm, N_EXPERTS),
        in_specs=[row(d), row(d), row(N_EXPERTS),
                  pl.BlockSpec((1, d, D_EXPERT), lambda i, e: (e, 0, 0)),
                  pl.BlockSpec((1, d, D_EXPERT), lambda i, e: (e, 0, 0)),
                  pl.BlockSpec((1, D_EXPERT, d), lambda i, e: (e, 0, 0)),
                  vec, vec],
        out_specs=[row(d), row(d)],
        out_shape=[jax.ShapeDtypeStruct((n, d), F32), jax.ShapeDtypeStruct((n, d), BF16)],
        scratch_shapes=[pltpu.VMEM((tm, d), F32)],
        compiler_params=_cparams(("parallel", "arbitrary")),
        name="moe",
    )(xb, xf, gate, wg, wu, wd, lg, lb)


def _rope_tables(seq):
    pos = jnp.arange(seq, dtype=F32)[:, None]
    lane = np.arange(LANES)

    def table(half, lo, hi):
        inv = ROPE_THETA ** (-jnp.arange(half, dtype=F32) / half)
        ang = pos * inv[None, :]
        idx = (lane - lo) % half
        active = (lane >= lo) & (lane < hi)
        first = ((lane - lo) % (2 * half)) < half
        cos = jnp.where(active[None, :], jnp.cos(ang)[:, idx], 1.0)
        sin = jnp.where(active[None, :], jnp.sin(ang)[:, idx] * np.where(first, -1.0, 1.0)[None, :], 0.0)
        return cos.astype(F32), sin.astype(F32)

    c64, s64 = table(B_HDIM // 2, 0, LANES)
    cq, sq = table(A_ROPE // 2, A_NOPE, A_NOPE + A_ROPE)
    ck, sk = table(A_ROPE // 2, 0, A_ROPE)
    return (c64, s64), (cq, sq, ck, sk)


def _kv_variants(w):
    z = jnp.zeros((w.shape[0], 64), w.dtype)
    k0, k1 = w[:, :64], w[:, 64:]
    return jnp.concatenate([k0, z, z, k0, k1, z, z, k1], axis=1)


def _layer_weights(l, w_in, a_w_uq, a_w_ukv):
    d = D_MODEL
    pts = [int(p) for p in np.cumsum(SPLIT_SIZES)[:-1]]
    (a_q, a_kv, a_kpe, b_q, b_k, b_v, b_qi, b_ki, b_wi, c_q, c_k, c_v, gates) = jnp.split(w_in[l], pts, axis=1)
    zeros = lambda n: jnp.zeros((d, n), F32)
    w_mla = jnp.concatenate([a_q, a_kv, a_kpe, zeros(LANES - A_ROPE)], axis=1)
    ki2 = jnp.concatenate([b_ki, zeros(64), zeros(64), b_ki], axis=1)
    wi_scale = (IDX_HEADS * IDX_DIM) ** -0.5
    w_dsa = jnp.concatenate([b_q * B_HDIM ** -0.5, _kv_variants(b_k), _kv_variants(b_v), b_qi, ki2,
                             b_wi * wi_scale, zeros(LANES - IDX_HEADS)], axis=1)
    w_swa = jnp.concatenate([c_q * C_HDIM ** -0.5, _kv_variants(c_k), _kv_variants(c_v)], axis=1)

    uq = a_w_uq[l].reshape(A_Q_RANK, A_HEADS, A_NOPE + A_ROPE) * (A_NOPE + A_ROPE) ** -0.5
    wq = jnp.concatenate([uq, jnp.zeros((A_Q_RANK, A_HEADS, LANES - A_NOPE - A_ROPE), F32)], axis=2)
    ukv = a_w_ukv[l].reshape(A_KV_RANK, A_HEADS, A_NOPE + A_VDIM)
    zk = jnp.zeros((A_KV_RANK, A_HEADS, 64), F32)
    wk = jnp.concatenate([ukv[:, :, :A_NOPE], zk], axis=2)
    v = ukv[:, :, A_NOPE:]
    even = (jnp.arange(A_HEADS) % 2 == 0)[None, :, None]
    wv = jnp.concatenate([jnp.where(even, v, 0.0), jnp.where(even, 0.0, v)], axis=2)
    wkv = jnp.concatenate([wk.reshape(A_KV_RANK, -1), wv.reshape(A_KV_RANK, -1)], axis=1)
    bf = lambda a: a.astype(BF16)
    return dict(w_mla=bf(w_mla), w_dsa=bf(w_dsa), w_swa=bf(w_swa), w_gate=bf(gates),
                wq=bf(wq.reshape(A_Q_RANK, -1)), wkv=bf(wkv))


DSA_SEGS = ((0, 512, "rope"), (512, 512, "rope"), (1024, 512, "plain"),
            (1536, 256, "rope"), (1792, 256, "rope"), (2048, 128, "plain"))
DSA_DTYPES = (BF16, BF16, BF16, BF16, BF16, F32)
SWA_SEGS = ((0, 512, "rope"), (512, 512, "rope"), (1024, 512, "plain"))
SWA_DTYPES = (BF16, BF16, BF16)
GATE_SEGS = tuple((c * D_MODEL, D_MODEL, "sigmoid") for c in range(3))


def kernel(x, ln_in_g, ln_in_b, w_in, a_q_ln_g, a_kv_ln_g, a_w_uq, a_w_ukv, c_sinks, w_br_a, w_br_b, w_br_c,
           w_out, ln1_g, ln1_b, w_router, router_bias, w_exp_gate, w_exp_up, w_exp_down, ln2_g, ln2_b):
    batch, seq, d = x.shape
    assert d == D_MODEL and seq % TM == 0 and seq % TQ == 0 and TQ == TK
    n = batch * seq
    (c64, s64), mla_tabs = _rope_tables(seq)
    xf, xb = _ln_call(x.reshape(n, d), ln_in_g, ln_in_b)
    wr_t = w_router.T
    rb = router_bias.reshape(N_EXPERTS, 1)
    bf = lambda a: a.astype(BF16)
    vec = lambda a: a.reshape(1, -1)
    for l in range(DEPTH):
        w = _layer_weights(l, w_in, a_w_uq, a_w_ukv)
        qa, ka, va = _mla_proj_call(xb, w["w_mla"], vec(a_q_ln_g[l]), vec(a_kv_ln_g[l]), w["wq"], w["wkv"],
                                    mla_tabs, seq)
        o_a = _mla_attn_call(qa, ka, va, batch, seq)
        bq, bkk, bvv, bqi, bki, bwi = _proj_call(xb, w["w_dsa"], c64, s64, DSA_SEGS, DSA_DTYPES, "dsa_proj", seq)
        o_b = _dsa_attn_call(bq, bkk, bvv, bqi, bki, bwi, batch, seq)
        cq_, ckk, cvv = _proj_call(xb, w["w_swa"], c64, s64, SWA_SEGS, SWA_DTYPES, "swa_proj", seq)
        o_c = _swa_attn_call(c_sinks[l], cq_, ckk, cvv, batch, seq)
        g0, g1, g2 = _proj_call(xb, w["w_gate"], c64, s64, GATE_SEGS, (BF16,) * 3, "gate_proj", seq)
        xf, xb = _merge_call(o_a, o_b, o_c, g0, g1, g2, xf, bf(w_br_a[l]), bf(w_br_b[l]), bf(w_br_c[l]), bf(w_out[l]),
                             vec(ln1_g[l]), vec(ln1_b[l]))
        gate = _router_call(xf, wr_t, rb).T
        xf, xb = _moe_call(xb, xf, gate, bf(w_exp_gate[l]), bf(w_exp_up[l]), bf(w_exp_down[l]),
                           vec(ln2_g[l]), vec(ln2_b[l]))
    return xf.reshape(batch, seq, d)
```

```python
import functools

import jax
import jax.numpy as jnp
import numpy as np
from jax import lax
from jax.experimental import pallas as pl
from jax.experimental.pallas import tpu as pltpu

F32 = jnp.float32
BF16 = jnp.bfloat16
I32 = jnp.int32

D_MODEL = 1024
DEPTH = 2
CHUNK = 64
ROPE_THETA = 10000.0
LN_EPS = 1e-5
RMS_EPS = 1e-6
NEG = -1e30

A_HEADS, A_NOPE, A_ROPE, A_VDIM, A_Q_RANK, A_KV_RANK = 8, 64, 32, 64, 384, 256
B_HEADS, B_KV_HEADS, B_HDIM, IDX_HEADS, IDX_DIM, TOPK_MAX = 8, 2, 64, 4, 64, 256
C_HEADS, C_KV_HEADS, C_HDIM, WINDOW = 8, 2, 64, 128
WIN_CHUNKS = WINDOW // CHUNK
N_EXPERTS, N_GROUPS, TOP_K, D_EXPERT = 16, 4, 2, 512
EXPERTS_PER_GROUP = N_EXPERTS // N_GROUPS
ALPHA = (2.0 * DEPTH) ** 0.25

SPLIT_SIZES = (
    A_Q_RANK, A_KV_RANK, A_ROPE,
    B_HEADS * B_HDIM, B_KV_HEADS * B_HDIM, B_KV_HEADS * B_HDIM,
    IDX_HEADS * IDX_DIM, IDX_DIM, IDX_HEADS,
    C_HEADS * C_HDIM, C_KV_HEADS * C_HDIM, C_KV_HEADS * C_HDIM,
    3 * D_MODEL,
)

LANES = 128
SUBLANES = 8
TM = 512
TQ = 256
TK = 256
SWA_KB = WIN_CHUNKS * CHUNK
TM_MOE = 512
TM_CMB = 256
WI_ROWS = 16
VMEM_LIMIT = 56 * 1024 * 1024
INT_MIN = -(2 ** 31)


def _cparams(sem):
    return pltpu.CompilerParams(dimension_semantics=sem, vmem_limit_bytes=VMEM_LIMIT)


def _dot(a, b):
    return jnp.dot(a, b, preferred_element_type=F32)


def _dot_nt(a, b):
    return lax.dot_general(a, b, (((1,), (1,)), ((), ())), preferred_element_type=F32)


def _layer_norm(z, g, b):
    mu = jnp.mean(z, axis=-1, keepdims=True)
    zc = z - mu
    var = jnp.mean(zc * zc, axis=-1, keepdims=True)
    return zc * lax.rsqrt(var + LN_EPS) * g + b


def _ln_kernel(x_ref, g_ref, b_ref, of_ref, ob_ref):
    y = _layer_norm(x_ref[...], g_ref[...], b_ref[...])
    of_ref[...] = y
    ob_ref[...] = y.astype(BF16)


def _ln_call(x, g, b):
    n, d = x.shape
    row = pl.BlockSpec((TM, d), lambda i: (i, 0))
    vec = pl.BlockSpec((1, d), lambda i: (0, 0))
    return pl.pallas_call(
        _ln_kernel,
        grid=(n // TM,),
        in_specs=[row, vec, vec],
        out_specs=[row, row],
        out_shape=[jax.ShapeDtypeStruct((n, d), F32), jax.ShapeDtypeStruct((n, d), BF16)],
        compiler_params=_cparams(("parallel",)),
        name="ln_in",
    )(x, g.reshape(1, d), b.reshape(1, d))


def _rope_lanes(y, cos, sin, lane, half):
    sw = jnp.where((lane & (2 * half - 1)) < half, pltpu.roll(y, LANES - half, 1), pltpu.roll(y, half, 1))
    return y * cos + sw * sin


def _qkv_proj_kernel(tok_segs, feat_segs, x_ref, w_ref, wt_ref, cos_ref, sin_ref, cost_ref, sint_ref, *out_refs):
    x = x_ref[...]
    n_tok = len(tok_segs)
    cos = cos_ref[...]
    sin = sin_ref[...]
    lane = lax.broadcasted_iota(I32, cos.shape, 1)
    for (start, width, rope), o_ref in zip(tok_segs, out_refs[:n_tok]):
        y = _dot(x, w_ref[:, start:start + width])
        for c in range(width // LANES):
            yc = y[:, c * LANES:(c + 1) * LANES]
            if rope:
                yc = _rope_lanes(yc, cos, sin, lane, B_HDIM // 2)
            o_ref[:, c * LANES:(c + 1) * LANES] = yc.astype(o_ref.dtype)
    ct = cost_ref[...]
    st = sint_ref[...]
    half = B_HDIM // 2
    for (start, rows, rope, kb), o_ref in zip(feat_segs, out_refs[n_tok:]):
        yt = _dot_nt(wt_ref[start:start + rows, :], x)
        if kb == "wide":
            n_heads = rows // B_HDIM
            rep = n_heads // (LANES // B_HDIM)
            zero = jnp.zeros((B_HDIM, TQ), o_ref.dtype)
            for h in range(n_heads):
                x0 = yt[h * B_HDIM:h * B_HDIM + half]
                x1 = yt[h * B_HDIM + half:(h + 1) * B_HDIM]
                r0 = (x0 * ct - x1 * st).astype(o_ref.dtype)
                r1 = (x1 * ct + x0 * st).astype(o_ref.dtype)
                g = h // rep
                for c in range(yt.shape[1] // TQ):
                    cs = slice(c * TQ, (c + 1) * TQ)
                    hs = slice(h * TQ, (h + 1) * TQ)
                    o_ref[c, g * B_HDIM:g * B_HDIM + half, hs] = r0[:, cs]
                    o_ref[c, g * B_HDIM + half:(g + 1) * B_HDIM, hs] = r1[:, cs]
                    o_ref[c, (1 - g) * B_HDIM:(2 - g) * B_HDIM, hs] = zero
        elif rope:
            for r in range(rows // B_HDIM):
                x0 = yt[r * B_HDIM:r * B_HDIM + half]
                x1 = yt[r * B_HDIM + half:(r + 1) * B_HDIM]
                o_ref[r * B_HDIM:r * B_HDIM + half, :] = (x0 * ct - x1 * st).astype(o_ref.dtype)
                o_ref[r * B_HDIM + half:(r + 1) * B_HDIM, :] = (x1 * ct + x0 * st).astype(o_ref.dtype)
        elif kb:
            for c in range(yt.shape[1] // kb):
                o_ref[c] = yt[:, c * kb:(c + 1) * kb].astype(o_ref.dtype)
        else:
            o_ref[...] = yt.astype(o_ref.dtype)


def _qkv_proj_call(xb, w, wt, tabs, tok_segs, tok_dtypes, feat_segs, feat_dtypes, name, seq):
    n, d = xb.shape
    tpb = seq // TM
    cos, sin, cost, sint = tabs
    full = lambda a: pl.BlockSpec(a.shape, lambda i: (0, 0))
    out_specs, out_shape = [], []
    for (start, width, rope), dt in zip(tok_segs, tok_dtypes):
        out_specs.append(pl.BlockSpec((TM, width), lambda i: (i, 0)))
        out_shape.append(jax.ShapeDtypeStruct((n, width), dt))
    for (start, rows, rope, kb), dt in zip(feat_segs, feat_dtypes):
        if kb == "wide":
            wide = (rows // B_HDIM) * TQ
            out_specs.append(pl.BlockSpec((TM // TQ, LANES, wide), lambda i: (i, 0, 0)))
            out_shape.append(jax.ShapeDtypeStruct((n // TQ, LANES, wide), dt))
        elif kb:
            out_specs.append(pl.BlockSpec((TM // kb, rows, kb), lambda i: (i, 0, 0)))
            out_shape.append(jax.ShapeDtypeStruct((n // kb, rows, kb), dt))
        else:
            out_specs.append(pl.BlockSpec((rows, TM), lambda i: (0, i)))
            out_shape.append(jax.ShapeDtypeStruct((rows, n), dt))
    return pl.pallas_call(
        functools.partial(_qkv_proj_kernel, tok_segs, feat_segs),
        grid=(n // TM,),
        in_specs=[pl.BlockSpec((TM, d), lambda i: (i, 0)), full(w), full(wt),
                  pl.BlockSpec((TM, LANES), lambda i: (i % tpb, 0)),
                  pl.BlockSpec((TM, LANES), lambda i: (i % tpb, 0)),
                  pl.BlockSpec((B_HDIM // 2, TM), lambda i: (0, i % tpb)),
                  pl.BlockSpec((B_HDIM // 2, TM), lambda i: (0, i % tpb))],
        out_specs=out_specs,
        out_shape=out_shape,
        compiler_params=_cparams(("parallel",)),
        name=name,
    )(xb, w, wt, cos, sin, cost, sint)


def _gate_proj_kernel(x_ref, w_ref, *out_refs):
    x = x_ref[...]
    for c, o_ref in enumerate(out_refs):
        y = _dot(x, w_ref[:, c * D_MODEL:(c + 1) * D_MODEL])
        o_ref[...] = jax.nn.sigmoid(y).astype(o_ref.dtype)


def _gate_proj_call(xb, w):
    n, d = xb.shape
    row = pl.BlockSpec((TM, d), lambda i: (i, 0))
    return pl.pallas_call(
        _gate_proj_kernel,
        grid=(n // TM,),
        in_specs=[row, pl.BlockSpec(w.shape, lambda i: (0, 0))],
        out_specs=[row] * 3,
        out_shape=[jax.ShapeDtypeStruct((n, d), BF16)] * 3,
        compiler_params=_cparams(("parallel",)),
        name="gate_proj",
    )(xb, w)


def _mla_proj_kernel(x_ref, w1_ref, qg_ref, kvg_ref, wqt_ref, wk_ref, wvt_ref,
                     cqt_ref, sqt_ref, ck_ref, sk_ref, qt_ref, k_ref, vt_ref):
    y = _dot(x_ref[...], w1_ref[...])
    ql = y[:, :A_Q_RANK]
    kvl = y[:, A_Q_RANK:A_Q_RANK + A_KV_RANK]
    kpe = y[:, A_Q_RANK + A_KV_RANK:]
    qn = (ql * lax.rsqrt(jnp.mean(ql * ql, axis=-1, keepdims=True) + RMS_EPS) * qg_ref[...]).astype(BF16)
    kvn = (kvl * lax.rsqrt(jnp.mean(kvl * kvl, axis=-1, keepdims=True) + RMS_EPS) * kvg_ref[...]).astype(BF16)
    half = A_ROPE // 2
    qt = _dot_nt(wqt_ref[...], qn)
    ct = cqt_ref[...]
    st = sqt_ref[...]
    for h in range(A_HEADS):
        base = h * LANES
        r0, r1, r2 = base + A_NOPE, base + A_NOPE + half, base + A_NOPE + A_ROPE
        x0 = qt[r0:r1]
        x1 = qt[r1:r2]
        qt_ref[base:r0, :] = qt[base:r0].astype(BF16)
        qt_ref[r0:r1, :] = (x0 * ct - x1 * st).astype(BF16)
        qt_ref[r1:r2, :] = (x1 * ct + x0 * st).astype(BF16)
        qt_ref[r2:base + LANES, :] = qt[r2:base + LANES].astype(BF16)
    kn = _dot(kvn, wk_ref[...])
    lane = lax.broadcasted_iota(I32, kpe.shape, 1)
    kpe_sw = jnp.where(lane < half, pltpu.roll(kpe, LANES - half, 1), pltpu.roll(kpe, half, 1))
    kpe_p = pltpu.roll(kpe * ck_ref[...] + kpe_sw * sk_ref[...], A_NOPE, 1)
    for h in range(A_HEADS):
        sl = slice(h * LANES, (h + 1) * LANES)
        k_ref[:, sl] = (kn[:, sl] + kpe_p).astype(BF16)
    vt = _dot_nt(wvt_ref[...], kvn)
    for c in range(vt.shape[1] // TK):
        vt_ref[c] = vt[:, c * TK:(c + 1) * TK].astype(BF16)


def _mla_proj_call(xb, w1, qg, kvg, wqt, wk, wvt, tabs, seq):
    n, d = xb.shape
    tpb = seq // TM
    cqt, sqt, ck, sk = tabs
    full = lambda a: pl.BlockSpec(a.shape, lambda i: (0, 0))
    ttab = pl.BlockSpec((A_ROPE // 2, TM), lambda i: (0, i % tpb))
    ltab = pl.BlockSpec((TM, LANES), lambda i: (i % tpb, 0))
    hw = A_HEADS * LANES
    vw = A_HEADS * A_VDIM
    return pl.pallas_call(
        _mla_proj_kernel,
        grid=(n // TM,),
        in_specs=[pl.BlockSpec((TM, d), lambda i: (i, 0)), full(w1), full(qg), full(kvg), full(wqt), full(wk),
                  full(wvt), ttab, ttab, ltab, ltab],
        out_specs=[pl.BlockSpec((hw, TM), lambda i: (0, i)),
                   pl.BlockSpec((TM, hw), lambda i: (i, 0)),
                   pl.BlockSpec((TM // TK, vw, TK), lambda i: (i, 0, 0))],
        out_shape=[jax.ShapeDtypeStruct((hw, n), BF16), jax.ShapeDtypeStruct((n, hw), BF16),
                   jax.ShapeDtypeStruct((n // TK, vw, TK), BF16)],
        compiler_params=_cparams(("parallel",)),
        name="mla_proj",
    )(xb, w1, qg, kvg, wqt, wk, wvt, cqt, sqt, ck, sk)


def _chunk_of(pos):
    return lax.shift_right_logical(pos, 6)


def _diag_visible():
    k = lax.broadcasted_iota(I32, (TK, TQ), 0)
    q = lax.broadcasted_iota(I32, (TK, TQ), 1)
    return _chunk_of(k) <= _chunk_of(q)


def _online_step(h, s, vt, m_ref, l_ref, acc_ref):
    m_prev = m_ref[h]
    m_new = jnp.maximum(m_prev, jnp.max(s, axis=0, keepdims=True))
    alpha = jnp.exp(m_prev - m_new)
    p = jnp.exp(s - m_new)
    l_ref[h] = alpha * l_ref[h] + jnp.sum(p, axis=0, keepdims=True)
    acc_ref[h] = alpha * acc_ref[h] + _dot(vt, p.astype(BF16))
    m_ref[h] = m_new


def _init_state(m_ref, l_ref, acc_ref):
    m_ref[...] = jnp.full(m_ref.shape, NEG, F32)
    l_ref[...] = jnp.zeros(l_ref.shape, F32)
    acc_ref[...] = jnp.zeros(acc_ref.shape, F32)


def _write_heads(o_ref, l_ref, acc_ref, n_heads):
    for j in range(n_heads // 2):
        ot = jnp.concatenate([acc_ref[2 * j] / l_ref[2 * j], acc_ref[2 * j + 1] / l_ref[2 * j + 1]], axis=0)
        o_ref[:, j * LANES:(j + 1) * LANES] = ot.T.astype(o_ref.dtype)


def _mla_attn_kernel(qt_ref, k_ref, vt_ref, o_ref, m_ref, l_ref, acc_ref):
    i = pl.program_id(1)
    _init_state(m_ref, l_ref, acc_ref)
    diag = _diag_visible()

    def block(j, masked):
        rows = pl.ds(pl.multiple_of(j * TK, TK), TK)
        vt = vt_ref[j]
        ss = [_dot(k_ref[rows, h * LANES:(h + 1) * LANES], qt_ref[h * LANES:(h + 1) * LANES, :])
              for h in range(A_HEADS)]
        for h in range(A_HEADS):
            s = jnp.where(diag, ss[h], NEG) if masked else ss[h]
            _online_step(h, s, vt[h * A_VDIM:(h + 1) * A_VDIM], m_ref, l_ref, acc_ref)

    def body(j, c):
        block(j, False)
        return c

    lax.fori_loop(0, i, body, 0)
    block(i, True)
    _write_heads(o_ref, l_ref, acc_ref, A_HEADS)


def _attn_scratch(n_heads, dv):
    return [pltpu.VMEM((n_heads, 1, TQ), F32), pltpu.VMEM((n_heads, 1, TQ), F32),
            pltpu.VMEM((n_heads, dv, TQ), F32)]


def _mla_attn_call(qt, k, vt, batch, seq):
    nq = seq // TQ
    nkb = seq // TK
    hw = A_HEADS * LANES
    vw = A_HEADS * A_VDIM
    return pl.pallas_call(
        _mla_attn_kernel,
        grid=(batch, nq),
        in_specs=[pl.BlockSpec((hw, TQ), lambda b, i: (0, b * nq + i)),
                  pl.BlockSpec((seq, hw), lambda b, i: (b, 0)),
                  pl.BlockSpec((nkb, vw, TK), lambda b, i: (b, 0, 0))],
        out_specs=pl.BlockSpec((TQ, vw), lambda b, i: (b * nq + i, 0)),
        out_shape=jax.ShapeDtypeStruct((batch * seq, vw), BF16),
        scratch_shapes=_attn_scratch(A_HEADS, A_VDIM),
        compiler_params=_cparams(("parallel", "arbitrary")),
        name="mla_attn",
    )(qt, k, vt)


def _dsa_attn_kernel(n_sel, qw_ref, k_ref, vt_ref, qit_ref, ki_ref, wt_ref, o_ref,
                     key_ref, bias_ref, j_ref, m_ref, l_ref, acc_ref):
    i = pl.program_id(1)
    nblk = i + 1
    diag = _diag_visible()
    kpos = lax.broadcasted_iota(I32, (TK, TQ), 0)
    w = wt_ref[...]

    def score_block(jb, c):
        ki = ki_ref[pl.ds(pl.multiple_of(jb * TK, TK), TK), :]
        sc = jnp.zeros((TK, TQ), F32)
        for h in range(IDX_HEADS):
            rel = jnp.maximum(_dot(ki, qit_ref[h * LANES:(h + 1) * LANES, :]), 0.0)
            sc = sc + w[h:h + 1, :] * rel
        sc = jnp.where(jnp.logical_or(jb < i, diag), sc, NEG)
        bits = lax.bitcast_convert_type(sc + 0.0, I32)
        key_ref[jb] = jnp.where(bits < 0, bits ^ 0x7FFFFFFF, bits)
        return c

    lax.fori_loop(0, nblk, score_block, 0)

    def count(pred):
        def body(jb, acc):
            c = pred(key_ref[jb], jb * TK + kpos).astype(I32)
            return acc + jnp.sum(c.reshape(TK // SUBLANES, SUBLANES, TQ), axis=0)
        acc = lax.fori_loop(0, nblk, body, jnp.zeros((SUBLANES, TQ), I32))
        return jnp.sum(acc, axis=0, keepdims=True)

    c0 = count(lambda k, g: k >= 0)
    t0 = jnp.where(c0 >= n_sel, 0, INT_MIN).astype(I32)

    def value_step(s, t):
        cand = t | lax.shift_left(jnp.int32(1), 30 - s)
        c = count(lambda k, g: k >= cand)
        return jnp.where(c >= n_sel, cand, t)

    t = lax.fori_loop(0, 31, value_step, t0)

    c_gt = count(lambda k, g: k > t)
    c_eq = count(lambda k, g: k == t)
    need = n_sel - c_gt
    j_ref[...] = jnp.full(j_ref.shape, 2 ** 30, I32)

    @pl.when(jnp.max(c_eq - need) > 0)
    def _():
        def index_step(s, jv):
            cand = jv | lax.shift_left(jnp.int32(1), 14 - s)
            c = count(lambda k, g: jnp.logical_and(k == t, g < cand))
            return jnp.where(c < need, cand, jv)
        j_ref[...] = lax.fori_loop(0, 15, index_step, jnp.zeros(j_ref.shape, I32))

    jv = j_ref[...]

    def bias_block(jb, c):
        k = key_ref[jb]
        g = jb * TK + kpos
        sel = jnp.logical_or(k > t, jnp.logical_and(k == t, g <= jv))
        vis = jnp.logical_or(jb < i, diag)
        bias_ref[jb] = jnp.where(jnp.logical_and(sel, vis), 0.0, NEG)
        return c

    lax.fori_loop(0, nblk, bias_block, 0)

    _init_state(m_ref, l_ref, acc_ref)

    def attn_block(jb, c):
        kb = k_ref[pl.ds(pl.multiple_of(jb * TK, TK), TK), :]
        vt = vt_ref[jb]
        bias = bias_ref[jb]
        s = _dot(kb, qw_ref[0])
        s = jnp.concatenate([s[:, h * TQ:(h + 1) * TQ] + bias for h in range(B_HEADS)], axis=1)
        m_prev = m_ref[...]
        m_new = jnp.maximum(m_prev, jnp.max(s, axis=0, keepdims=True))
        alpha = jnp.exp(m_prev - m_new)
        p = jnp.exp(s - m_new)
        l_ref[...] = alpha * l_ref[...] + jnp.sum(p, axis=0, keepdims=True)
        acc_ref[...] = alpha * acc_ref[...] + _gqa_pv([vt], p.astype(BF16), B_KV_HEADS, B_HDIM)
        m_ref[...] = m_new
        return c

    lax.fori_loop(0, nblk, attn_block, 0)
    _write_wide(o_ref, acc_ref[...] / l_ref[...], B_HEADS)


def _gqa_pv(vts, pb, n_groups, hdim):
    gw = pb.shape[1] // n_groups
    outs = []
    for g in range(n_groups):
        o = None
        k0 = 0
        for vt in vts:
            kc = vt.shape[1]
            part = _dot(vt[g * hdim:(g + 1) * hdim], pb[k0:k0 + kc, g * gw:(g + 1) * gw])
            o = part if o is None else o + part
            k0 += kc
        outs.append(o)
    return jnp.concatenate(outs, axis=1)


def _write_wide(o_ref, ow, n_heads):
    for j in range(n_heads // 2):
        ot = jnp.concatenate([ow[:, 2 * j * TQ:(2 * j + 1) * TQ], ow[:, (2 * j + 1) * TQ:(2 * j + 2) * TQ]], axis=0)
        o_ref[:, j * LANES:(j + 1) * LANES] = ot.T.astype(o_ref.dtype)


def _wide_scratch(n_heads, dv):
    return [pltpu.VMEM((1, n_heads * TQ), F32), pltpu.VMEM((1, n_heads * TQ), F32),
            pltpu.VMEM((dv, n_heads * TQ), F32)]


def _dsa_attn_call(qw, k, vt, qit, ki, wt, batch, seq):
    nq = seq // TQ
    nkb = seq // TK
    n_sel = min(TOPK_MAX, seq // 4)
    qcol = lambda rows: pl.BlockSpec((rows, TQ), lambda b, i: (0, b * nq + i))
    brow = lambda wdt: pl.BlockSpec((seq, wdt), lambda b, i: (b, 0))
    ow = B_HEADS * B_HDIM
    return pl.pallas_call(
        functools.partial(_dsa_attn_kernel, n_sel),
        grid=(batch, nq),
        in_specs=[pl.BlockSpec((1,) + qw.shape[1:], lambda b, i: (b * nq + i, 0, 0)), brow(k.shape[1]),
                  pl.BlockSpec((nkb,) + vt.shape[1:], lambda b, i: (b, 0, 0)),
                  qcol(qit.shape[0]), brow(ki.shape[1]), qcol(wt.shape[0])],
        out_specs=pl.BlockSpec((TQ, ow), lambda b, i: (b * nq + i, 0)),
        out_shape=jax.ShapeDtypeStruct((batch * seq, ow), BF16),
        scratch_shapes=[pltpu.VMEM((nkb, TK, TQ), I32), pltpu.VMEM((nkb, TK, TQ), F32),
                        pltpu.VMEM((1, TQ), I32)] + _wide_scratch(B_HEADS, B_HDIM),
        compiler_params=_cparams(("parallel", "arbitrary")),
        name="dsa_attn",
    )(qw, k, vt, qit, ki, wt)


SWA_SLAB = TQ + WIN_CHUNKS * CHUNK


def _swa_attn_kernel(sink_ref, qw_ref, k_ref, vt_ref, o_ref):
    i = pl.program_id(1)
    s0 = pl.multiple_of(jnp.maximum(i * TQ - SWA_KB, 0), SWA_KB)
    kslab = k_ref[pl.ds(s0, SWA_SLAB), :]
    kc = _chunk_of(s0 + lax.broadcasted_iota(I32, (SWA_SLAB, TQ), 0))
    qc = _chunk_of(i * TQ + lax.broadcasted_iota(I32, (SWA_SLAB, TQ), 1))
    valid = jnp.logical_and(kc <= qc, kc >= qc - WIN_CHUNKS)
    b0 = lax.div(s0, SWA_KB)
    vts = [vt_ref[b0 + c] for c in range(SWA_SLAB // SWA_KB)]
    s = _dot(kslab, qw_ref[0])
    s = jnp.concatenate([jnp.where(valid, s[:, h * TQ:(h + 1) * TQ], NEG) for h in range(C_HEADS)], axis=1)
    sink = jnp.concatenate([jnp.full((1, TQ), sink_ref[h], F32) for h in range(C_HEADS)], axis=1)
    m = jnp.maximum(jnp.max(s, axis=0, keepdims=True), sink)
    p = jnp.exp(s - m)
    l = jnp.sum(p, axis=0, keepdims=True) + jnp.exp(sink - m)
    _write_wide(o_ref, _gqa_pv(vts, p.astype(BF16), C_KV_HEADS, C_HDIM) / l, C_HEADS)


def _swa_attn_call(sinks, qw, k, vt, batch, seq):
    nq = seq // TQ
    nkb = seq // SWA_KB
    ow = C_HEADS * C_HDIM
    return pl.pallas_call(
        _swa_attn_kernel,
        grid=(batch, nq),
        in_specs=[pl.BlockSpec(memory_space=pltpu.SMEM),
                  pl.BlockSpec((1,) + qw.shape[1:], lambda b, i: (b * nq + i, 0, 0)),
                  pl.BlockSpec((seq, k.shape[1]), lambda b, i: (b, 0)),
                  pl.BlockSpec((nkb,) + vt.shape[1:], lambda b, i: (b, 0, 0))],
        out_specs=pl.BlockSpec((TQ, ow), lambda b, i: (b * nq + i, 0)),
        out_shape=jax.ShapeDtypeStruct((batch * seq, ow), BF16),
        compiler_params=_cparams(("parallel", "arbitrary")),
        name="swa_attn",
    )(sinks, qw, k, vt)


def _merge_kernel(oa_ref, ob_ref, oc_ref, ga_ref, gb_ref, gc_ref, x_ref, wa_ref, wb_ref, wc_ref, wo_ref,
                  lg_ref, lb_ref, of_ref, ob16_ref):
    merged = (ga_ref[...].astype(F32) * _dot(oa_ref[...], wa_ref[...])
              + gb_ref[...].astype(F32) * _dot(ob_ref[...], wb_ref[...])
              + gc_ref[...].astype(F32) * _dot(oc_ref[...], wc_ref[...]))
    mix = _dot(merged.astype(BF16), wo_ref[...])
    y = _layer_norm(ALPHA * x_ref[...] + mix, lg_ref[...], lb_ref[...])
    of_ref[...] = y
    ob16_ref[...] = y.astype(BF16)


def _merge_call(oa, ob, oc, ga, gb, gc, x, wa, wb, wc, wo, lg, lb):
    n, d = x.shape
    row = lambda wdt: pl.BlockSpec((TM, wdt), lambda i: (i, 0))
    full = lambda a: pl.BlockSpec(a.shape, lambda i: (0, 0))
    return pl.pallas_call(
        _merge_kernel,
        grid=(n // TM,),
        in_specs=[row(oa.shape[1]), row(ob.shape[1]), row(oc.shape[1]), row(d), row(d), row(d), row(d),
                  full(wa), full(wb), full(wc), full(wo), full(lg), full(lb)],
        out_specs=[row(d), row(d)],
        out_shape=[jax.ShapeDtypeStruct((n, d), F32), jax.ShapeDtypeStruct((n, d), BF16)],
        compiler_params=_cparams(("parallel",)),
        name="merge",
    )(oa, ob, oc, ga, gb, gc, x, wa, wb, wc, wo, lg, lb)


def _router_kernel(x_ref, wr_ref, rb_ref, idx_ref, wgt_ref, cnt_ref, mask_ref, carry_ref):
    @pl.when(pl.program_id(0) == 0)
    def _():
        carry_ref[...] = jnp.zeros(carry_ref.shape, F32)

    logits =lax.dot_general(wr_ref[...], x_ref[...], (((1,), (1,)), ((), ())),
                             precision=lax.Precision.HIGHEST, preferred_element_type=F32)
    scores = jax.nn.sigmoid(logits)
    biased = scores + rb_ref[...]
    sc = [scores[e:e + 1, :] for e in range(N_EXPERTS)]
    bi = [biased[e:e + 1, :] for e in range(N_EXPERTS)]
    epg = EXPERTS_PER_GROUP
    gs = []
    for g in range(N_GROUPS):
        b = bi[g * epg:(g + 1) * epg]
        best = None
        for a in range(epg):
            for c in range(a + 1, epg):
                pair = b[a] + b[c]
                best = pair if best is None else jnp.maximum(best, pair)
        gs.append(best)
    gsel = []
    for g in range(N_GROUPS):
        ok = None
        for o in range(N_GROUPS):
            if o == g:
                continue
            cond = gs[g] > gs[o] if o < g else gs[g] >= gs[o]
            ok = cond if ok is None else jnp.logical_and(ok, cond)
        gsel.append(ok)
    chosen = []
    for e in range(N_EXPERTS):
        g = e // epg
        rank = jnp.zeros_like(bi[e], dtype=I32)
        for o in range(g * epg, (g + 1) * epg):
            if o == e:
                continue
            ahead = bi[o] >= bi[e] if o < e else bi[o] > bi[e]
            rank = rank + ahead.astype(I32)
        chosen.append(jnp.logical_and(gsel[g], rank < TOP_K))
    picked = [jnp.where(chosen[e], sc[e], 0.0) for e in range(N_EXPERTS)]
    total = picked[0]
    for e in range(1, N_EXPERTS):
        total = total + picked[e]
    for e in range(N_EXPERTS):
        mask_ref[e:e + 1, :] = chosen[e].astype(F32)
    mask = mask_ref[...]
    tm = mask.shape[1]
    upper = (lax.broadcasted_iota(I32, (tm, tm), 0) <= lax.broadcasted_iota(I32, (tm, tm), 1)).astype(BF16)
    pos = carry_ref[...] + _dot(mask.astype(BF16), upper) - 1.0
    carry_ref[...] += jnp.sum(mask, axis=1, keepdims=True)
    cnt_ref[...] = jnp.broadcast_to(carry_ref[...], cnt_ref.shape)
    zero_i = jnp.zeros_like(bi[0], dtype=I32)
    zero_f = jnp.zeros_like(bi[0])
    seen = None
    e_lo, e_hi, p_lo, p_hi, w_lo, w_hi = zero_i, zero_i, zero_f, zero_f, zero_f, zero_f
    for e in range(N_EXPERTS):
        is_lo = chosen[e] if seen is None else jnp.logical_and(chosen[e], jnp.logical_not(seen))
        is_hi = jnp.zeros_like(chosen[e]) if seen is None else jnp.logical_and(chosen[e], seen)
        seen = chosen[e] if seen is None else jnp.logical_or(seen, chosen[e])
        pe = pos[e:e + 1, :]
        we = picked[e] / total
        e_lo = jnp.where(is_lo, e, e_lo)
        e_hi = jnp.where(is_hi, e, e_hi)
        p_lo = jnp.where(is_lo, pe, p_lo)
        p_hi = jnp.where(is_hi, pe, p_hi)
        w_lo = jnp.where(is_lo, we, w_lo)
        w_hi = jnp.where(is_hi, we, w_hi)
    idx_ref[...] = jnp.zeros(idx_ref.shape, I32)
    idx_ref[0:1, :] = e_lo
    idx_ref[1:2, :] = e_hi
    idx_ref[2:3, :] = p_lo.astype(I32)
    idx_ref[3:4, :] = p_hi.astype(I32)
    wgt_ref[...] = jnp.zeros(wgt_ref.shape, F32)
    wgt_ref[0:1, :] = w_lo
    wgt_ref[1:2, :] = w_hi


def _router_call(x, wr_t, rb):
    n, d = x.shape
    return pl.pallas_call(
        _router_kernel,
        grid=(n // TM,),
        in_specs=[pl.BlockSpec((TM, d), lambda i: (i, 0)),
                  pl.BlockSpec((N_EXPERTS, d), lambda i: (0, 0)),
                  pl.BlockSpec((N_EXPERTS, 1), lambda i: (0, 0))],
        out_specs=[pl.BlockSpec((SUBLANES, TM), lambda i: (0, i)),
                   pl.BlockSpec((SUBLANES, TM), lambda i: (0, i)),
                   pl.BlockSpec((N_EXPERTS, LANES), lambda i: (0, 0))],
        out_shape=[jax.ShapeDtypeStruct((SUBLANES, n), I32), jax.ShapeDtypeStruct((SUBLANES, n), F32),
                   jax.ShapeDtypeStruct((N_EXPERTS, LANES), F32)],
        scratch_shapes=[pltpu.VMEM((N_EXPERTS, TM), F32), pltpu.VMEM((N_EXPERTS, 1), F32)],
        compiler_params=_cparams(("arbitrary",)),
        name="router",
    )(x, wr_t, rb)


def _row_copy(src, src_row, dst, dst_row, sem):
    return pltpu.make_async_copy(src.at[pl.ds(src_row, 1)], dst.at[pl.ds(dst_row, 1)], sem)


def _dispatch_kernel(dlo_ref, dhi_ref, pad_ref, x_hbm, xs_hbm, zero_ref, sem):
    i = pl.program_id(0)

    @pl.when(i == 0)
    def _():
        zero_ref[...] = jnp.zeros(zero_ref.shape, F32)
        for j in range(2 * N_EXPERTS):
            @pl.when(pad_ref[j] >= 0)
            def _():
                first = pl.multiple_of(pad_ref[j], TM_MOE)
                cp = pltpu.make_async_copy(zero_ref, xs_hbm.at[pl.ds(first, TM_MOE)], sem.at[2])
                cp.start()
                cp.wait()

    base = i * TM

    def body(r, c):
        t = base + r
        _row_copy(x_hbm, t, xs_hbm, dlo_ref[t], sem.at[0]).start()
        _row_copy(x_hbm, t, xs_hbm, dhi_ref[t], sem.at[1]).start()
        return c

    lax.fori_loop(0, TM, body, 0, unroll=8)
    pltpu.make_async_copy(x_hbm.at[pl.ds(0, TM)], xs_hbm.at[pl.ds(0, TM)], sem.at[0]).wait()
    pltpu.make_async_copy(x_hbm.at[pl.ds(0, TM)], xs_hbm.at[pl.ds(0, TM)], sem.at[1]).wait()


def _dispatch_call(dlo, dhi, pad_start, xf, n_rows):
    n, d = xf.shape
    return pl.pallas_call(
        _dispatch_kernel,
        grid_spec=pltpu.PrefetchScalarGridSpec(
            num_scalar_prefetch=3,
            grid=(n // TM,),
            in_specs=[pl.BlockSpec(memory_space=pl.ANY)],
            out_specs=pl.BlockSpec(memory_space=pl.ANY),
            scratch_shapes=[pltpu.VMEM((TM_MOE, d), F32), pltpu.SemaphoreType.DMA((3,))]),
        out_shape=jax.ShapeDtypeStruct((n_rows, d), F32),
        compiler_params=_cparams(("arbitrary",)),
        name="moe_dispatch",
    )(dlo, dhi, pad_start, xf)


def _experts_kernel(te_ref, nt_ref, xs_ref, wg_ref, wu_ref, wd_ref, ys_ref):
    used = pl.program_id(0) < nt_ref[0]

    @pl.when(used)
    def _():
        x = xs_ref[...].astype(BF16)
        a = _dot(x, wg_ref[0])
        u = _dot(x, wu_ref[0])
        h = a * jax.nn.sigmoid(a) * u
        ys_ref[...] = _dot(h.astype(BF16), wd_ref[0])

    @pl.when(jnp.logical_not(used))
    def _():
        ys_ref[...] = jnp.zeros(ys_ref.shape, F32)


def _experts_call(tile_expert, n_tiles, xs, wg, wu, wd):
    n_rows, d = xs.shape
    tile = lambda i, te, nt: (jnp.minimum(i, nt[0] - 1), 0)
    return pl.pallas_call(
        _experts_kernel,
        grid_spec=pltpu.PrefetchScalarGridSpec(
            num_scalar_prefetch=2,
            grid=(n_rows // TM_MOE,),
            in_specs=[pl.BlockSpec((TM_MOE, d), tile),
                      pl.BlockSpec((1, d, D_EXPERT), lambda i, te, nt: (te[i], 0, 0)),
                      pl.BlockSpec((1, d, D_EXPERT), lambda i, te, nt: (te[i], 0, 0)),
                      pl.BlockSpec((1, D_EXPERT, d), lambda i, te, nt: (te[i], 0, 0))],
            out_specs=pl.BlockSpec((TM_MOE, d), lambda i, te, nt: (i, 0))),
        out_shape=jax.ShapeDtypeStruct((n_rows, d), F32),
        compiler_params=_cparams(("arbitrary",)),
        name="moe_experts",
    )(tile_expert, n_tiles, xs, wg, wu, wd)


def _combine_kernel(dlo_ref, dhi_ref, ys_hbm, xf_ref, w_ref, lg_ref, lb_ref, of_ref, ob_ref, a_ref, b_ref, sem):
    i = pl.program_id(0)
    n = pl.num_programs(0)

    def issue(tile, slot):
        base = tile * TM_CMB

        def body(r, c):
            t = base + r
            _row_copy(ys_hbm, dlo_ref[t], a_ref.at[slot], r, sem.at[0, slot]).start()
            _row_copy(ys_hbm, dhi_ref[t], b_ref.at[slot], r, sem.at[1, slot]).start()
            return c

        lax.fori_loop(0, TM_CMB, body, 0, unroll=8)

    @pl.when(i == 0)
    def _():
        issue(0, 0)

    @pl.when(i + 1 < n)
    def _():
        issue(i + 1, (i + 1) % 2)

    slot = i % 2
    pltpu.make_async_copy(ys_hbm.at[pl.ds(0, TM_CMB)], a_ref.at[slot], sem.at[0, slot]).wait()
    pltpu.make_async_copy(ys_hbm.at[pl.ds(0, TM_CMB)], b_ref.at[slot], sem.at[1, slot]).wait()
    w = w_ref[...]
    ffn = w[:, 0:1] * a_ref[slot] + w[:, 1:2] * b_ref[slot]
    y = _layer_norm(ALPHA * xf_ref[...] + ffn, lg_ref[...], lb_ref[...])
    of_ref[...] = y
    ob_ref[...] = y.astype(BF16)


def _combine_call(dlo, dhi, ys, xf, w2, lg, lb):
    n, d = xf.shape
    row = lambda wdt: pl.BlockSpec((TM_CMB, wdt), lambda i, a, b: (i, 0))
    vec = pl.BlockSpec((1, d), lambda i, a, b: (0, 0))
    return pl.pallas_call(
        _combine_kernel,
        grid_spec=pltpu.PrefetchScalarGridSpec(
            num_scalar_prefetch=2,
            grid=(n // TM_CMB,),
            in_specs=[pl.BlockSpec(memory_space=pl.ANY), row(d), row(2), vec, vec],
            out_specs=[row(d), row(d)],
            scratch_shapes=[pltpu.VMEM((2, TM_CMB, d), F32), pltpu.VMEM((2, TM_CMB, d), F32),
                            pltpu.SemaphoreType.DMA((2, 2))]),
        out_shape=[jax.ShapeDtypeStruct((n, d), F32), jax.ShapeDtypeStruct((n, d), BF16)],
        compiler_params=_cparams(("arbitrary",)),
        name="moe_combine",
    )(dlo, dhi, ys, xf, w2, lg, lb)


def _moe(xf, wr_t, rb, wg, wu, wd, lg, lb):
    n, d = xf.shape
    idx, wgt, cnt = _router_call(xf, wr_t, rb)
    counts = cnt[:, 0].astype(I32)
    padded = (counts + TM_MOE - 1) // TM_MOE * TM_MOE
    ends = jnp.cumsum(padded)
    starts = ends - padded
    max_tiles = (TOP_K * n) // TM_MOE + N_EXPERTS
    n_tiles = (ends[-1] // TM_MOE).astype(I32).reshape(1)
    tile_ids = jnp.arange(max_tiles, dtype=I32)
    tile_expert = jnp.minimum(jnp.sum((tile_ids[:, None] >= (ends // TM_MOE)[None, :]).astype(I32), axis=1),
                              N_EXPERTS - 1)
    onehot = lambda e: (e[None, :] == jnp.arange(N_EXPERTS, dtype=I32)[:, None])
    dlo = jnp.sum(jnp.where(onehot(idx[0]), starts[:, None], 0), axis=0) + idx[2]
    dhi = jnp.sum(jnp.where(onehot(idx[1]), starts[:, None], 0), axis=0) + idx[3]
    tail = n_tiles[0] + jnp.arange(N_EXPERTS, dtype=I32)
    pad_start = jnp.concatenate([jnp.where(counts % TM_MOE != 0, ends - TM_MOE, -1),
                                 jnp.where(tail < max_tiles, tail * TM_MOE, -1)]).astype(I32)
    xs = _dispatch_call(dlo, dhi, pad_start, xf, max_tiles * TM_MOE)
    ys = _experts_call(tile_expert, n_tiles, xs, wg, wu, wd)
    return _combine_call(dlo, dhi, ys, xf, wgt[:2].T, lg, lb)


def _rope_tables(seq):
    pos = jnp.arange(seq, dtype=F32)[:, None]
    lane = np.arange(LANES)

    def angles(half):
        inv = ROPE_THETA ** (-jnp.arange(half, dtype=F32) / half)
        return pos * inv[None, :]

    def lane_table(half, lo, hi):
        ang = angles(half)
        idx = (lane - lo) % half
        active = (lane >= lo) & (lane < hi)
        first = ((lane - lo) % (2 * half)) < half
        cos = jnp.where(active[None, :], jnp.cos(ang)[:, idx], 1.0)
        sin = jnp.where(active[None, :], jnp.sin(ang)[:, idx] * np.where(first, -1.0, 1.0)[None, :], 0.0)
        return cos.astype(F32), sin.astype(F32)

    def feat_table(half):
        ang = angles(half)
        return jnp.cos(ang).T.astype(F32), jnp.sin(ang).T.astype(F32)

    c64, s64 = lane_table(B_HDIM // 2, 0, LANES)
    c64t, s64t = feat_table(B_HDIM // 2)
    ck, sk = lane_table(A_ROPE // 2, 0, A_ROPE)
    cqt, sqt = feat_table(A_ROPE // 2)
    return (c64, s64, c64t, s64t), (cqt, sqt, ck, sk)


def _layer_weights(l, w_in, a_w_uq, a_w_ukv):
    d = D_MODEL
    pts = [int(p) for p in np.cumsum(SPLIT_SIZES)[:-1]]
    (a_q, a_kv, a_kpe, b_q, b_k, b_v, b_qi, b_ki, b_wi, c_q, c_k, c_v, gates) = jnp.split(w_in[l], pts, axis=1)
    zeros = lambda n: jnp.zeros((d, n), F32)
    bf = lambda a: a.astype(BF16)
    w_mla = jnp.concatenate([a_q, a_kv, a_kpe, zeros(LANES - A_ROPE)], axis=1)
    dsa_w = jnp.concatenate([b_k, b_ki, zeros(LANES - IDX_DIM)], axis=1)
    qi_t = jnp.concatenate([b_qi.reshape(d, IDX_HEADS, IDX_DIM),
                            jnp.zeros((d, IDX_HEADS, LANES - IDX_DIM), F32)], axis=2).reshape(d, -1).T
    wi_t = jnp.concatenate([b_wi * (IDX_HEADS * IDX_DIM) ** -0.5, zeros(WI_ROWS - IDX_HEADS)], axis=1).T
    dsa_wt = jnp.concatenate([(b_q * B_HDIM ** -0.5).T, b_v.T, qi_t, wi_t], axis=0)
    swa_wt = jnp.concatenate([(c_q * C_HDIM ** -0.5).T, c_v.T], axis=0)
    uq = a_w_uq[l].reshape(A_Q_RANK, A_HEADS, A_NOPE + A_ROPE) * (A_NOPE + A_ROPE) ** -0.5
    wq = jnp.concatenate([uq, jnp.zeros((A_Q_RANK, A_HEADS, LANES - A_NOPE - A_ROPE), F32)], axis=2)
    ukv = a_w_ukv[l].reshape(A_KV_RANK, A_HEADS, A_NOPE + A_VDIM)
    wk = jnp.concatenate([ukv[:, :, :A_NOPE], jnp.zeros((A_KV_RANK, A_HEADS, LANES - A_NOPE), F32)], axis=2)
    wv = ukv[:, :, A_NOPE:]
    return dict(w_mla=bf(w_mla), dsa_w=bf(dsa_w), dsa_wt=bf(dsa_wt), swa_w=bf(c_k), swa_wt=bf(swa_wt),
                w_gate=bf(gates), wqt=bf(wq.reshape(A_Q_RANK, -1).T), wk=bf(wk.reshape(A_KV_RANK, -1)),
                wvt=bf(wv.reshape(A_KV_RANK, -1).T))


QW = B_HEADS * B_HDIM
DSA_TOK = ((0, LANES, True), (LANES, LANES, True))
DSA_FEAT = ((0, QW, True, "wide"), (QW, LANES, False, TK), (QW + LANES, IDX_HEADS * LANES, True, 0),
            (QW + LANES + IDX_HEADS * LANES, WI_ROWS, False, 0))
SWA_TOK = ((0, LANES, True),)
SWA_FEAT = ((0, QW, True, "wide"), (QW, LANES, False, SWA_KB))


def kernel(x, ln_in_g, ln_in_b, w_in, a_q_ln_g, a_kv_ln_g, a_w_uq, a_w_ukv, c_sinks, w_br_a, w_br_b, w_br_c,
           w_out, ln1_g, ln1_b, w_router, router_bias, w_exp_gate, w_exp_up, w_exp_down, ln2_g, ln2_b):
    batch, seq, d = x.shape
    assert d == D_MODEL and seq % TM == 0 and seq % TQ == 0 and TQ == TK
    n = batch * seq
    tabs64, mla_tabs = _rope_tables(seq)
    xf, xb = _ln_call(x.reshape(n, d), ln_in_g, ln_in_b)
    wr_t = w_router.T
    rb = router_bias.reshape(N_EXPERTS, 1)
    bf = lambda a: a.astype(BF16)
    vec = lambda a: a.reshape(1, -1)
    for l in range(DEPTH):
        w = _layer_weights(l, w_in, a_w_uq, a_w_ukv)
        qa, ka, va = _mla_proj_call(xb, w["w_mla"], vec(a_q_ln_g[l]), vec(a_kv_ln_g[l]), w["wqt"], w["wk"],
                                    w["wvt"], mla_tabs, seq)
        o_a = _mla_attn_call(qa, ka, va, batch, seq)
        bk, bki, bq, bv, bqi, bwi = _qkv_proj_call(xb, w["dsa_w"], w["dsa_wt"], tabs64, DSA_TOK, (BF16, BF16),
                                                   DSA_FEAT, (BF16, BF16, BF16, F32), "dsa_proj", seq)
        o_b = _dsa_attn_call(bq, bk, bv, bqi, bki, bwi, batch, seq)
        ck_, cq_, cv_ = _qkv_proj_call(xb, w["swa_w"], w["swa_wt"], tabs64, SWA_TOK, (BF16,),
                                       SWA_FEAT, (BF16, BF16), "swa_proj", seq)
        o_c = _swa_attn_call(c_sinks[l], cq_, ck_, cv_, batch, seq)
        g0, g1, g2 = _gate_proj_call(xb, w["w_gate"])
        xf, xb = _merge_call(o_a, o_b, o_c, g0, g1, g2, xf, bf(w_br_a[l]), bf(w_br_b[l]), bf(w_br_c[l]),
                             bf(w_out[l]), vec(ln1_g[l]), vec(ln1_b[l]))
        xf, xb = _moe(xf, wr_t, rb, bf(w_exp_gate[l]), bf(w_exp_up[l]), bf(w_exp_down[l]),
                      vec(ln2_g[l]), vec(ln2_b[l]))
    return xf.reshape(batch, seq, d)
```

```python
import functools

import jax
import jax.numpy as jnp
import numpy as np
from jax import lax
from jax.experimental import pallas as pl
from jax.experimental.pallas import tpu as pltpu

F32 = jnp.float32
BF16 = jnp.bfloat16
I32 = jnp.int32

D_MODEL = 1024
DEPTH = 2
CHUNK = 64
ROPE_THETA = 10000.0
LN_EPS = 1e-5
RMS_EPS = 1e-6
NEG = -1e30

A_HEADS, A_NOPE, A_ROPE, A_VDIM, A_Q_RANK, A_KV_RANK = 8, 64, 32, 64, 384, 256
B_HEADS, B_KV_HEADS, B_HDIM, IDX_HEADS, IDX_DIM, TOPK_MAX = 8, 2, 64, 4, 64, 256
C_HEADS, C_KV_HEADS, C_HDIM, WINDOW = 8, 2, 64, 128
WIN_CHUNKS = WINDOW // CHUNK
N_EXPERTS, N_GROUPS, TOP_K, D_EXPERT = 16, 4, 2, 512
EXPERTS_PER_GROUP = N_EXPERTS // N_GROUPS
ALPHA = (2.0 * DEPTH) ** 0.25
LOG2E = 1.4426950408889634

SPLIT_SIZES = (
    A_Q_RANK, A_KV_RANK, A_ROPE,
    B_HEADS * B_HDIM, B_KV_HEADS * B_HDIM, B_KV_HEADS * B_HDIM,
    IDX_HEADS * IDX_DIM, IDX_DIM, IDX_HEADS,
    C_HEADS * C_HDIM, C_KV_HEADS * C_HDIM, C_KV_HEADS * C_HDIM,
    3 * D_MODEL,
)

LANES = 128
SUBLANES = 8
TM = 512
TQ = 256
TK = 256
SWA_KB = WIN_CHUNKS * CHUNK
TM_MOE = 512
TM_CMB = 256
ROW_UNROLL = 8
WI_ROWS = 16
VMEM_LIMIT = 56 * 1024 * 1024
INT_MIN = -(2 ** 31)


def _cparams(sem):
    return pltpu.CompilerParams(dimension_semantics=sem, vmem_limit_bytes=VMEM_LIMIT)


def _dot(a, b):
    return jnp.dot(a, b, preferred_element_type=F32)


def _dot_nt(a, b):
    return lax.dot_general(a, b, (((1,), (1,)), ((), ())), preferred_element_type=F32)


def _layer_norm(z, g, b):
    mu = jnp.mean(z, axis=-1, keepdims=True)
    zc = z - mu
    var = jnp.mean(zc * zc, axis=-1, keepdims=True)
    return zc * lax.rsqrt(var + LN_EPS) * g + b


def _ln_kernel(x_ref, g_ref, b_ref, of_ref, ob_ref):
    y = _layer_norm(x_ref[...], g_ref[...], b_ref[...])
    of_ref[...] = y
    ob_ref[...] = y.astype(BF16)


def _ln_call(x, g, b):
    n, d = x.shape
    row = pl.BlockSpec((TM, d), lambda i: (i, 0))
    vec = pl.BlockSpec((1, d), lambda i: (0, 0))
    return pl.pallas_call(
        _ln_kernel,
        grid=(n // TM,),
        in_specs=[row, vec, vec],
        out_specs=[row, row],
        out_shape=[jax.ShapeDtypeStruct((n, d), F32), jax.ShapeDtypeStruct((n, d), BF16)],
        compiler_params=_cparams(("parallel",)),
        name="ln_in",
    )(x, g.reshape(1, d), b.reshape(1, d))


def _rope_lanes(y, cos, sin, lane, half):
    sw = jnp.where((lane & (2 * half - 1)) < half, pltpu.roll(y, LANES - half, 1), pltpu.roll(y, half, 1))
    return y * cos + sw * sin


def _qkv_proj_kernel(tok_segs, feat_segs, x_ref, w_ref, wt_ref, cos_ref, sin_ref, cost_ref, sint_ref, *out_refs):
    x = x_ref[...]
    n_tok = len(tok_segs)
    cos = cos_ref[...]
    sin = sin_ref[...]
    lane = lax.broadcasted_iota(I32, cos.shape, 1)
    for (start, width, rope), o_ref in zip(tok_segs, out_refs[:n_tok]):
        y = _dot(x, w_ref[:, start:start + width])
        for c in range(width // LANES):
            yc = y[:, c * LANES:(c + 1) * LANES]
            if rope:
                yc = _rope_lanes(yc, cos, sin, lane, B_HDIM // 2)
            o_ref[:, c * LANES:(c + 1) * LANES] = yc.astype(o_ref.dtype)
    ct = cost_ref[...]
    st = sint_ref[...]
    half = B_HDIM // 2
    for (start, rows, rope, kb), o_ref in zip(feat_segs, out_refs[n_tok:]):
        yt = _dot_nt(wt_ref[start:start + rows, :], x)
        if kb == "wide":
            n_heads = rows // B_HDIM
            rep = n_heads // (LANES // B_HDIM)
            zero = jnp.zeros((B_HDIM, TQ), o_ref.dtype)
            for h in range(n_heads):
                x0 = yt[h * B_HDIM:h * B_HDIM + half]
                x1 = yt[h * B_HDIM + half:(h + 1) * B_HDIM]
                r0 = (x0 * ct - x1 * st).astype(o_ref.dtype)
                r1 = (x1 * ct + x0 * st).astype(o_ref.dtype)
                g = h // rep
                for c in range(yt.shape[1] // TQ):
                    cs = slice(c * TQ, (c + 1) * TQ)
                    hs = slice(h * TQ, (h + 1) * TQ)
                    o_ref[c, g * B_HDIM:g * B_HDIM + half, hs] = r0[:, cs]
                    o_ref[c, g * B_HDIM + half:(g + 1) * B_HDIM, hs] = r1[:, cs]
                    o_ref[c, (1 - g) * B_HDIM:(2 - g) * B_HDIM, hs] = zero
        elif rope:
            for r in range(rows // B_HDIM):
                x0 = yt[r * B_HDIM:r * B_HDIM + half]
                x1 = yt[r * B_HDIM + half:(r + 1) * B_HDIM]
                o_ref[r * B_HDIM:r * B_HDIM + half, :] = (x0 * ct - x1 * st).astype(o_ref.dtype)
                o_ref[r * B_HDIM + half:(r + 1) * B_HDIM, :] = (x1 * ct + x0 * st).astype(o_ref.dtype)
        elif kb:
            for c in range(yt.shape[1] // kb):
                o_ref[c] = yt[:, c * kb:(c + 1) * kb].astype(o_ref.dtype)
        else:
            o_ref[...] = yt.astype(o_ref.dtype)


def _qkv_proj_call(xb, w, wt, tabs, tok_segs, tok_dtypes, feat_segs, feat_dtypes, name, seq):
    n, d = xb.shape
    tpb = seq // TM
    cos, sin, cost, sint = tabs
    full = lambda a: pl.BlockSpec(a.shape, lambda i: (0, 0))
    out_specs, out_shape = [], []
    for (start, width, rope), dt in zip(tok_segs, tok_dtypes):
        out_specs.append(pl.BlockSpec((TM, width), lambda i: (i, 0)))
        out_shape.append(jax.ShapeDtypeStruct((n, width), dt))
    for (start, rows, rope, kb), dt in zip(feat_segs, feat_dtypes):
        if kb == "wide":
            wide = (rows // B_HDIM) * TQ
            out_specs.append(pl.BlockSpec((TM // TQ, LANES, wide), lambda i: (i, 0, 0)))
            out_shape.append(jax.ShapeDtypeStruct((n // TQ, LANES, wide), dt))
        elif kb:
            out_specs.append(pl.BlockSpec((TM // kb, rows, kb), lambda i: (i, 0, 0)))
            out_shape.append(jax.ShapeDtypeStruct((n // kb, rows, kb), dt))
        else:
            out_specs.append(pl.BlockSpec((rows, TM), lambda i: (0, i)))
            out_shape.append(jax.ShapeDtypeStruct((rows, n), dt))
    return pl.pallas_call(
        functools.partial(_qkv_proj_kernel, tok_segs, feat_segs),
        grid=(n // TM,),
        in_specs=[pl.BlockSpec((TM, d), lambda i: (i, 0)), full(w), full(wt),
                  pl.BlockSpec((TM, LANES), lambda i: (i % tpb, 0)),
                  pl.BlockSpec((TM, LANES), lambda i: (i % tpb, 0)),
                  pl.BlockSpec((B_HDIM // 2, TM), lambda i: (0, i % tpb)),
                  pl.BlockSpec((B_HDIM // 2, TM), lambda i: (0, i % tpb))],
        out_specs=out_specs,
        out_shape=out_shape,
        compiler_params=_cparams(("parallel",)),
        name=name,
    )(xb, w, wt, cos, sin, cost, sint)


def _gate_proj_kernel(x_ref, w_ref, *out_refs):
    x = x_ref[...]
    for c, o_ref in enumerate(out_refs):
        y = _dot(x, w_ref[:, c * D_MODEL:(c + 1) * D_MODEL])
        o_ref[...] = jax.nn.sigmoid(y).astype(o_ref.dtype)


def _gate_proj_call(xb, w):
    n, d = xb.shape
    row = pl.BlockSpec((TM, d), lambda i: (i, 0))
    return pl.pallas_call(
        _gate_proj_kernel,
        grid=(n // TM,),
        in_specs=[row, pl.BlockSpec(w.shape, lambda i: (0, 0))],
        out_specs=[row] * 3,
        out_shape=[jax.ShapeDtypeStruct((n, d), BF16)] * 3,
        compiler_params=_cparams(("parallel",)),
        name="gate_proj",
    )(xb, w)


def _mla_proj_kernel(x_ref, w1_ref, qg_ref, kvg_ref, wqt_ref, wk_ref, wvt_ref,
                     cqt_ref, sqt_ref, ck_ref, sk_ref, qt_ref, k_ref, vt_ref):
    y = _dot(x_ref[...], w1_ref[...])
    ql = y[:, :A_Q_RANK]
    kvl = y[:, A_Q_RANK:A_Q_RANK + A_KV_RANK]
    kpe = y[:, A_Q_RANK + A_KV_RANK:]
    qn = (ql * lax.rsqrt(jnp.mean(ql * ql, axis=-1, keepdims=True) + RMS_EPS) * qg_ref[...]).astype(BF16)
    kvn = (kvl * lax.rsqrt(jnp.mean(kvl * kvl, axis=-1, keepdims=True) + RMS_EPS) * kvg_ref[...]).astype(BF16)
    half = A_ROPE // 2
    qt = _dot_nt(wqt_ref[...], qn)
    ct = cqt_ref[...]
    st = sqt_ref[...]
    for h in range(A_HEADS):
        base = h * LANES
        r0, r1, r2 = base + A_NOPE, base + A_NOPE + half, base + A_NOPE + A_ROPE
        x0 = qt[r0:r1]
        x1 = qt[r1:r2]
        qt_ref[base:r0, :] = qt[base:r0].astype(BF16)
        qt_ref[r0:r1, :] = (x0 * ct - x1 * st).astype(BF16)
        qt_ref[r1:r2, :] = (x1 * ct + x0 * st).astype(BF16)
        qt_ref[r2:base + LANES, :] = qt[r2:base + LANES].astype(BF16)
    kn = _dot(kvn, wk_ref[...])
    lane = lax.broadcasted_iota(I32, kpe.shape, 1)
    kpe_sw = jnp.where(lane < half, pltpu.roll(kpe, LANES - half, 1), pltpu.roll(kpe, half, 1))
    kpe_p = pltpu.roll(kpe * ck_ref[...] + kpe_sw * sk_ref[...], A_NOPE, 1)
    for h in range(A_HEADS):
        sl = slice(h * LANES, (h + 1) * LANES)
        k_ref[:, sl] = (kn[:, sl] + kpe_p).astype(BF16)
    vt = _dot_nt(wvt_ref[...], kvn)
    for c in range(vt.shape[1] // TK):
        vt_ref[c] = vt[:, c * TK:(c + 1) * TK].astype(BF16)


def _mla_proj_call(xb, w1, qg, kvg, wqt, wk, wvt, tabs, seq):
    n, d = xb.shape
    tpb = seq // TM
    cqt, sqt, ck, sk = tabs
    full = lambda a: pl.BlockSpec(a.shape, lambda i: (0, 0))
    ttab = pl.BlockSpec((A_ROPE // 2, TM), lambda i: (0, i % tpb))
    ltab = pl.BlockSpec((TM, LANES), lambda i: (i % tpb, 0))
    hw = A_HEADS * LANES
    vw = A_HEADS * A_VDIM
    return pl.pallas_call(
        _mla_proj_kernel,
        grid=(n // TM,),
        in_specs=[pl.BlockSpec((TM, d), lambda i: (i, 0)), full(w1), full(qg), full(kvg), full(wqt), full(wk),
                  full(wvt), ttab, ttab, ltab, ltab],
        out_specs=[pl.BlockSpec((hw, TM), lambda i: (0, i)),
                   pl.BlockSpec((TM, hw), lambda i: (i, 0)),
                   pl.BlockSpec((TM // TK, vw, TK), lambda i: (i, 0, 0))],
        out_shape=[jax.ShapeDtypeStruct((hw, n), BF16), jax.ShapeDtypeStruct((n, hw), BF16),
                   jax.ShapeDtypeStruct((n // TK, vw, TK), BF16)],
        compiler_params=_cparams(("parallel",)),
        name="mla_proj",
    )(xb, w1, qg, kvg, wqt, wk, wvt, cqt, sqt, ck, sk)


def _chunk_of(pos):
    return lax.shift_right_logical(pos, 6)


def _diag_visible():
    k = lax.broadcasted_iota(I32, (TK, TQ), 0)
    q = lax.broadcasted_iota(I32, (TK, TQ), 1)
    return _chunk_of(k) <= _chunk_of(q)


def _online_step(h, s, vt, m_ref, l_ref, acc_ref):
    m_prev = m_ref[h]
    m_new = jnp.maximum(m_prev, jnp.max(s, axis=0, keepdims=True))
    alpha = jnp.exp2(m_prev - m_new)
    p = jnp.exp2(s - m_new)
    l_ref[h] = alpha * l_ref[h] + jnp.sum(p, axis=0, keepdims=True)
    acc_ref[h] = alpha * acc_ref[h] + _dot(vt, p.astype(BF16))
    m_ref[h] = m_new


def _init_state(m_ref, l_ref, acc_ref):
    m_ref[...] = jnp.full(m_ref.shape, NEG, F32)
    l_ref[...] = jnp.zeros(l_ref.shape, F32)
    acc_ref[...] = jnp.zeros(acc_ref.shape, F32)


def _write_heads(o_ref, l_ref, acc_ref, n_heads):
    for j in range(n_heads // 2):
        ot = jnp.concatenate([acc_ref[2 * j] / l_ref[2 * j], acc_ref[2 * j + 1] / l_ref[2 * j + 1]], axis=0)
        o_ref[:, j * LANES:(j + 1) * LANES] = ot.T.astype(o_ref.dtype)


def _mla_attn_kernel(qt_ref, k_ref, vt_ref, o_ref, m_ref, l_ref, acc_ref):
    i = pl.program_id(1)
    _init_state(m_ref, l_ref, acc_ref)
    diag = _diag_visible()

    def block(j, masked):
        rows = pl.ds(pl.multiple_of(j * TK, TK), TK)
        vt = vt_ref[j]
        ss = [_dot(k_ref[rows, h * LANES:(h + 1) * LANES], qt_ref[h * LANES:(h + 1) * LANES, :])
              for h in range(A_HEADS)]
        for h in range(A_HEADS):
            s = jnp.where(diag, ss[h], NEG) if masked else ss[h]
            _online_step(h, s, vt[h * A_VDIM:(h + 1) * A_VDIM], m_ref, l_ref, acc_ref)

    def body(j, c):
        block(j, False)
        return c

    lax.fori_loop(0, i, body, 0)
    block(i, True)
    _write_heads(o_ref, l_ref, acc_ref, A_HEADS)


def _attn_scratch(n_heads, dv):
    return [pltpu.VMEM((n_heads, 1, TQ), F32), pltpu.VMEM((n_heads, 1, TQ), F32),
            pltpu.VMEM((n_heads, dv, TQ), F32)]


def _mla_attn_call(qt, k, vt, batch, seq):
    nq = seq // TQ
    nkb = seq // TK
    hw = A_HEADS * LANES
    vw = A_HEADS * A_VDIM
    return pl.pallas_call(
        _mla_attn_kernel,
        grid=(batch, nq),
        in_specs=[pl.BlockSpec((hw, TQ), lambda b, i: (0, b * nq + i)),
                  pl.BlockSpec((seq, hw), lambda b, i: (b, 0)),
                  pl.BlockSpec((nkb, vw, TK), lambda b, i: (b, 0, 0))],
        out_specs=pl.BlockSpec((TQ, vw), lambda b, i: (b * nq + i, 0)),
        out_shape=jax.ShapeDtypeStruct((batch * seq, vw), BF16),
        scratch_shapes=_attn_scratch(A_HEADS, A_VDIM),
        compiler_params=_cparams(("parallel", "arbitrary")),
        name="mla_attn",
    )(qt, k, vt)


def _dsa_attn_kernel(n_sel, qw_ref, k_ref, vt_ref, qit_ref, ki_ref, wt_ref, o_ref,
                     key_ref, bias_ref, j_ref, m_ref, l_ref, acc_ref):
    i = pl.program_id(1)
    nblk = i + 1
    diag = _diag_visible()
    kpos = lax.broadcasted_iota(I32, (TK, TQ), 0)
    w = wt_ref[...]

    def score_block(jb, c):
        ki = ki_ref[pl.ds(pl.multiple_of(jb * TK, TK), TK), :]
        sc = jnp.zeros((TK, TQ), F32)
        for h in range(IDX_HEADS):
            rel = jnp.maximum(_dot(ki, qit_ref[h * LANES:(h + 1) * LANES, :]), 0.0)
            sc = sc + w[h:h + 1, :] * rel
        sc = jnp.where(jnp.logical_or(jb < i, diag), sc, NEG)
        bits = lax.bitcast_convert_type(sc + 0.0, I32)
        key_ref[jb] = jnp.where(bits < 0, bits ^ 0x7FFFFFFF, bits)
        return c

    lax.fori_loop(0, nblk, score_block, 0)

    def count(pred):
        def body(jb, acc):
            c = pred(key_ref[jb], jb * TK + kpos).astype(I32)
            return acc + jnp.sum(c.reshape(TK // SUBLANES, SUBLANES, TQ), axis=0)
        acc = lax.fori_loop(0, nblk, body, jnp.zeros((SUBLANES, TQ), I32))
        return jnp.sum(acc, axis=0, keepdims=True)

    c0 = count(lambda k, g: k >= 0)
    t0 = jnp.where(c0 >= n_sel, 0, INT_MIN).astype(I32)

    def value_step(s, t):
        cand = t | lax.shift_left(jnp.int32(1), 30 - s)
        c = count(lambda k, g: k >= cand)
        return jnp.where(c >= n_sel, cand, t)

    t = lax.fori_loop(0, 31, value_step, t0)

    c_gt = count(lambda k, g: k > t)
    c_eq = count(lambda k, g: k == t)
    need = n_sel - c_gt
    j_ref[...] = jnp.full(j_ref.shape, 2 ** 30, I32)

    @pl.when(jnp.max(c_eq - need) > 0)
    def _():
        def index_step(s, jv):
            cand = jv | lax.shift_left(jnp.int32(1), 14 - s)
            c = count(lambda k, g: jnp.logical_and(k == t, g < cand))
            return jnp.where(c < need, cand, jv)
        j_ref[...] = lax.fori_loop(0, 15, index_step, jnp.zeros(j_ref.shape, I32))

    jv = j_ref[...]

    def bias_block(jb, c):
        k = key_ref[jb]
        g = jb * TK + kpos
        sel = jnp.logical_or(k > t, jnp.logical_and(k == t, g <= jv))
        vis = jnp.logical_or(jb < i, diag)
        bias_ref[jb] = jnp.where(jnp.logical_and(sel, vis), 0.0, NEG)
        return c

    lax.fori_loop(0, nblk, bias_block, 0)

    _init_state(m_ref, l_ref, acc_ref)

    def attn_block(jb, c):
        kb = k_ref[pl.ds(pl.multiple_of(jb * TK, TK), TK), :]
        vt = vt_ref[jb]
        bias = bias_ref[jb]
        s = _dot(kb, qw_ref[0])
        s = jnp.concatenate([s[:, h * TQ:(h + 1) * TQ] + bias for h in range(B_HEADS)], axis=1)
        m_prev = m_ref[...]
        m_new = jnp.maximum(m_prev, jnp.max(s, axis=0, keepdims=True))
        alpha = jnp.exp2(m_prev - m_new)
        p = jnp.exp2(s - m_new)
        l_ref[...] = alpha * l_ref[...] + jnp.sum(p, axis=0, keepdims=True)
        acc_ref[...] = alpha * acc_ref[...] + _gqa_pv([vt], p.astype(BF16), B_KV_HEADS, B_HDIM)
        m_ref[...] = m_new
        return c

    lax.fori_loop(0, nblk, attn_block, 0)
    _write_wide(o_ref, acc_ref[...] / l_ref[...], B_HEADS)


def _gqa_pv(vts, pb, n_groups, hdim):
    gw = pb.shape[1] // n_groups
    outs = []
    for g in range(n_groups):
        o = None
        k0 = 0
        for vt in vts:
            kc = vt.shape[1]
            part = _dot(vt[g * hdim:(g + 1) * hdim], pb[k0:k0 + kc, g * gw:(g + 1) * gw])
            o = part if o is None else o + part
            k0 += kc
        outs.append(o)
    return jnp.concatenate(outs, axis=1)


def _write_wide(o_ref, ow, n_heads):
    for j in range(n_heads // 2):
        ot = jnp.concatenate([ow[:, 2 * j * TQ:(2 * j + 1) * TQ], ow[:, (2 * j + 1) * TQ:(2 * j + 2) * TQ]], axis=0)
        o_ref[:, j * LANES:(j + 1) * LANES] = ot.T.astype(o_ref.dtype)


def _wide_scratch(n_heads, dv):
    return [pltpu.VMEM((1, n_heads * TQ), F32), pltpu.VMEM((1, n_heads * TQ), F32),
            pltpu.VMEM((dv, n_heads * TQ), F32)]


def _dsa_attn_call(qw, k, vt, qit, ki, wt, batch, seq):
    nq = seq // TQ
    nkb = seq // TK
    n_sel = min(TOPK_MAX, seq // 4)
    qcol = lambda rows: pl.BlockSpec((rows, TQ), lambda b, i: (0, b * nq + i))
    brow = lambda wdt: pl.BlockSpec((seq, wdt), lambda b, i: (b, 0))
    ow = B_HEADS * B_HDIM
    return pl.pallas_call(
        functools.partial(_dsa_attn_kernel, n_sel),
        grid=(batch, nq),
        in_specs=[pl.BlockSpec((1,) + qw.shape[1:], lambda b, i: (b * nq + i, 0, 0)), brow(k.shape[1]),
                  pl.BlockSpec((nkb,) + vt.shape[1:], lambda b, i: (b, 0, 0)),
                  qcol(qit.shape[0]), brow(ki.shape[1]), qcol(wt.shape[0])],
        out_specs=pl.BlockSpec((TQ, ow), lambda b, i: (b * nq + i, 0)),
        out_shape=jax.ShapeDtypeStruct((batch * seq, ow), BF16),
        scratch_shapes=[pltpu.VMEM((nkb, TK, TQ), I32), pltpu.VMEM((nkb, TK, TQ), F32),
                        pltpu.VMEM((1, TQ), I32)] + _wide_scratch(B_HEADS, B_HDIM),
        compiler_params=_cparams(("parallel", "arbitrary")),
        name="dsa_attn",
    )(qw, k, vt, qit, ki, wt)


SWA_SLAB = TQ + WIN_CHUNKS * CHUNK


def _swa_attn_kernel(sink_ref, qw_ref, k_ref, vt_ref, o_ref):
    i = pl.program_id(1)
    s0 = pl.multiple_of(jnp.maximum(i * TQ - SWA_KB, 0), SWA_KB)
    kslab = k_ref[pl.ds(s0, SWA_SLAB), :]
    kc = _chunk_of(s0 + lax.broadcasted_iota(I32, (SWA_SLAB, TQ), 0))
    qc = _chunk_of(i * TQ + lax.broadcasted_iota(I32, (SWA_SLAB, TQ), 1))
    valid = jnp.logical_and(kc <= qc, kc >= qc - WIN_CHUNKS)
    b0 = lax.div(s0, SWA_KB)
    vts = [vt_ref[b0 + c] for c in range(SWA_SLAB // SWA_KB)]
    s = _dot(kslab, qw_ref[0])
    s = jnp.concatenate([jnp.where(valid, s[:, h * TQ:(h + 1) * TQ], NEG) for h in range(C_HEADS)], axis=1)
    sink = jnp.concatenate([jnp.full((1, TQ), sink_ref[h] * LOG2E, F32) for h in range(C_HEADS)], axis=1)
    m = jnp.maximum(jnp.max(s, axis=0, keepdims=True), sink)
    p = jnp.exp2(s - m)
    l = jnp.sum(p, axis=0, keepdims=True) + jnp.exp2(sink - m)
    _write_wide(o_ref, _gqa_pv(vts, p.astype(BF16), C_KV_HEADS, C_HDIM) / l, C_HEADS)


def _swa_attn_call(sinks, qw, k, vt, batch, seq):
    nq = seq // TQ
    nkb = seq // SWA_KB
    ow = C_HEADS * C_HDIM
    return pl.pallas_call(
        _swa_attn_kernel,
        grid=(batch, nq),
        in_specs=[pl.BlockSpec(memory_space=pltpu.SMEM),
                  pl.BlockSpec((1,) + qw.shape[1:], lambda b, i: (b * nq + i, 0, 0)),
                  pl.BlockSpec((seq, k.shape[1]), lambda b, i: (b, 0)),
                  pl.BlockSpec((nkb,) + vt.shape[1:], lambda b, i: (b, 0, 0))],
        out_specs=pl.BlockSpec((TQ, ow), lambda b, i: (b * nq + i, 0)),
        out_shape=jax.ShapeDtypeStruct((batch * seq, ow), BF16),
        compiler_params=_cparams(("parallel", "arbitrary")),
        name="swa_attn",
    )(sinks, qw, k, vt)


def _merge_kernel(oa_ref, ob_ref, oc_ref, ga_ref, gb_ref, gc_ref, x_ref, wa_ref, wb_ref, wc_ref, wo_ref,
                  lg_ref, lb_ref, of_ref, ob16_ref):
    merged = (ga_ref[...].astype(F32) * _dot(oa_ref[...], wa_ref[...])
              + gb_ref[...].astype(F32) * _dot(ob_ref[...], wb_ref[...])
              + gc_ref[...].astype(F32) * _dot(oc_ref[...], wc_ref[...]))
    mix = _dot(merged.astype(BF16), wo_ref[...])
    y = _layer_norm(ALPHA * x_ref[...] + mix, lg_ref[...], lb_ref[...])
    of_ref[...] = y
    ob16_ref[...] = y.astype(BF16)


def _merge_call(oa, ob, oc, ga, gb, gc, x, wa, wb, wc, wo, lg, lb):
    n, d = x.shape
    row = lambda wdt: pl.BlockSpec((TM, wdt), lambda i: (i, 0))
    full = lambda a: pl.BlockSpec(a.shape, lambda i: (0, 0))
    return pl.pallas_call(
        _merge_kernel,
        grid=(n // TM,),
        in_specs=[row(oa.shape[1]), row(ob.shape[1]), row(oc.shape[1]), row(d), row(d), row(d), row(d),
                  full(wa), full(wb), full(wc), full(wo), full(lg), full(lb)],
        out_specs=[row(d), row(d)],
        out_shape=[jax.ShapeDtypeStruct((n, d), F32), jax.ShapeDtypeStruct((n, d), BF16)],
        compiler_params=_cparams(("parallel",)),
        name="merge",
    )(oa, ob, oc, ga, gb, gc, x, wa, wb, wc, wo, lg, lb)


def _router_kernel(x_ref, wr_ref, rb_ref, idx_ref, wgt_ref, cnt_ref, mask_ref, carry_ref):
    @pl.when(pl.program_id(0) == 0)
    def _():
        carry_ref[...] = jnp.zeros(carry_ref.shape, F32)

    logits =lax.dot_general(wr_ref[...], x_ref[...], (((1,), (1,)), ((), ())),
                             precision=lax.Precision.HIGHEST, preferred_element_type=F32)
    scores = jax.nn.sigmoid(logits)
    biased = scores + rb_ref[...]
    sc = [scores[e:e + 1, :] for e in range(N_EXPERTS)]
    bi = [biased[e:e + 1, :] for e in range(N_EXPERTS)]
    epg = EXPERTS_PER_GROUP
    gs = []
    for g in range(N_GROUPS):
        b = bi[g * epg:(g + 1) * epg]
        best = None
        for a in range(epg):
            for c in range(a + 1, epg):
                pair = b[a] + b[c]
                best = pair if best is None else jnp.maximum(best, pair)
        gs.append(best)
    gsel = []
    for g in range(N_GROUPS):
        ok = None
        for o in range(N_GROUPS):
            if o == g:
                continue
            cond = gs[g] > gs[o] if o < g else gs[g] >= gs[o]
            ok = cond if ok is None else jnp.logical_and(ok, cond)
        gsel.append(ok)
    chosen = []
    for e in range(N_EXPERTS):
        g = e // epg
        rank = jnp.zeros_like(bi[e], dtype=I32)
        for o in range(g * epg, (g + 1) * epg):
            if o == e:
                continue
            ahead = bi[o] >= bi[e] if o < e else bi[o] > bi[e]
            rank = rank + ahead.astype(I32)
        chosen.append(jnp.logical_and(gsel[g], rank < TOP_K))
    picked = [jnp.where(chosen[e], sc[e], 0.0) for e in range(N_EXPERTS)]
    total = picked[0]
    for e in range(1, N_EXPERTS):
        total = total + picked[e]
    for e in range(N_EXPERTS):
        mask_ref[e:e + 1, :] = chosen[e].astype(F32)
    mask = mask_ref[...]
    tm = mask.shape[1]
    upper = (lax.broadcasted_iota(I32, (tm, tm), 0) <= lax.broadcasted_iota(I32, (tm, tm), 1)).astype(BF16)
    pos = carry_ref[...] + _dot(mask.astype(BF16), upper) - 1.0
    carry_ref[...] += jnp.sum(mask, axis=1, keepdims=True)
    cnt_ref[...] = jnp.broadcast_to(carry_ref[...], cnt_ref.shape)
    zero_i = jnp.zeros_like(bi[0], dtype=I32)
    zero_f = jnp.zeros_like(bi[0])
    seen = None
    e_lo, e_hi, p_lo, p_hi, w_lo, w_hi = zero_i, zero_i, zero_f, zero_f, zero_f, zero_f
    for e in range(N_EXPERTS):
        is_lo = chosen[e] if seen is None else jnp.logical_and(chosen[e], jnp.logical_not(seen))
        is_hi = jnp.zeros_like(chosen[e]) if seen is None else jnp.logical_and(chosen[e], seen)
        seen = chosen[e] if seen is None else jnp.logical_or(seen, chosen[e])
        pe = pos[e:e + 1, :]
        we = picked[e] / total
        e_lo = jnp.where(is_lo, e, e_lo)
        e_hi = jnp.where(is_hi, e, e_hi)
        p_lo = jnp.where(is_lo, pe, p_lo)
        p_hi = jnp.where(is_hi, pe, p_hi)
        w_lo = jnp.where(is_lo, we, w_lo)
        w_hi = jnp.where(is_hi, we, w_hi)
    idx_ref[...] = jnp.zeros(idx_ref.shape, I32)
    idx_ref[0:1, :] = e_lo
    idx_ref[1:2, :] = e_hi
    idx_ref[2:3, :] = p_lo.astype(I32)
    idx_ref[3:4, :] = p_hi.astype(I32)
    wgt_ref[...] = jnp.zeros(wgt_ref.shape, F32)
    wgt_ref[0:1, :] = w_lo
    wgt_ref[1:2, :] = w_hi


def _router_call(x, wr_t, rb):
    n, d = x.shape
    return pl.pallas_call(
        _router_kernel,
        grid=(n // TM,),
        in_specs=[pl.BlockSpec((TM, d), lambda i: (i, 0)),
                  pl.BlockSpec((N_EXPERTS, d), lambda i: (0, 0)),
                  pl.BlockSpec((N_EXPERTS, 1), lambda i: (0, 0))],
        out_specs=[pl.BlockSpec((SUBLANES, TM), lambda i: (0, i)),
                   pl.BlockSpec((SUBLANES, TM), lambda i: (0, i)),
                   pl.BlockSpec((N_EXPERTS, LANES), lambda i: (0, 0))],
        out_shape=[jax.ShapeDtypeStruct((SUBLANES, n), I32), jax.ShapeDtypeStruct((SUBLANES, n), F32),
                   jax.ShapeDtypeStruct((N_EXPERTS, LANES), F32)],
        scratch_shapes=[pltpu.VMEM((N_EXPERTS, TM), F32), pltpu.VMEM((N_EXPERTS, 1), F32)],
        compiler_params=_cparams(("arbitrary",)),
        name="router",
    )(x, wr_t, rb)


def _row_copy(src, src_row, dst, dst_row, sem):
    return pltpu.make_async_copy(src.at[pl.ds(src_row, 1)], dst.at[pl.ds(dst_row, 1)], sem)


def _dispatch_kernel(dlo_ref, dhi_ref, pad_ref, x_ref, xs_hbm, zero_ref, sem):
    i = pl.program_id(0)

    @pl.when(i == 0)
    def _():
        zero_ref[...] = jnp.zeros(zero_ref.shape, F32)
        for j in range(2 * N_EXPERTS):
            @pl.when(pad_ref[j] >= 0)
            def _():
                first = pl.multiple_of(pad_ref[j], TM_MOE)
                cp = pltpu.make_async_copy(zero_ref, xs_hbm.at[pl.ds(first, TM_MOE)], sem.at[2])
                cp.start()
                cp.wait()

    base = i * TM

    def body(g, c):
        for u in range(ROW_UNROLL):
            r = g * ROW_UNROLL + u
            t = base + r
            _row_copy(x_ref, r, xs_hbm, dlo_ref[t], sem.at[0]).start(priority=u % 2)
            _row_copy(x_ref, r, xs_hbm, dhi_ref[t], sem.at[1]).start(priority=(u + 1) % 2)
        return c

    lax.fori_loop(0, TM // ROW_UNROLL, body, 0)
    pltpu.make_async_copy(x_ref, xs_hbm.at[pl.ds(0, TM)], sem.at[0]).wait()
    pltpu.make_async_copy(x_ref, xs_hbm.at[pl.ds(0, TM)], sem.at[1]).wait()


def _dispatch_call(dlo, dhi, pad_start, xf, n_rows):
    n, d = xf.shape
    return pl.pallas_call(
        _dispatch_kernel,
        grid_spec=pltpu.PrefetchScalarGridSpec(
            num_scalar_prefetch=3,
            grid=(n // TM,),
            in_specs=[pl.BlockSpec((TM, d), lambda i, a, b, c: (i, 0))],
            out_specs=pl.BlockSpec(memory_space=pl.ANY),
            scratch_shapes=[pltpu.VMEM((TM_MOE, d), F32), pltpu.SemaphoreType.DMA((3,))]),
        out_shape=jax.ShapeDtypeStruct((n_rows, d), F32),
        compiler_params=_cparams(("arbitrary",)),
        name="moe_dispatch",
    )(dlo, dhi, pad_start, xf)


def _experts_kernel(te_ref, nt_ref, xs_ref, wg_ref, wu_ref, wd_ref, ys_ref, wgb_ref, wub_ref, wdb_ref):
    i = pl.program_id(0)
    used = i < nt_ref[0]

    @pl.when(jnp.logical_or(i == 0, te_ref[i] != te_ref[jnp.maximum(i - 1, 0)]))
    def _():
        wgb_ref[...] = wg_ref[0].astype(BF16)
        wub_ref[...] = wu_ref[0].astype(BF16)
        wdb_ref[...] = wd_ref[0].astype(BF16)

    @pl.when(used)
    def _():
        x = xs_ref[...].astype(BF16)
        a = _dot(x, wgb_ref[...])
        u = _dot(x, wub_ref[...])
        h = a * jax.nn.sigmoid(a) * u
        ys_ref[...] = _dot(h.astype(BF16), wdb_ref[...])

    @pl.when(jnp.logical_not(used))
    def _():
        ys_ref[...] = jnp.zeros(ys_ref.shape, F32)


def _experts_call(tile_expert, n_tiles, xs, wg, wu, wd):
    n_rows, d = xs.shape
    tile = lambda i, te, nt: (jnp.maximum(jnp.minimum(i, nt[0] - 1), 0), 0)
    return pl.pallas_call(
        _experts_kernel,
        grid_spec=pltpu.PrefetchScalarGridSpec(
            num_scalar_prefetch=2,
            grid=(n_rows // TM_MOE,),
            in_specs=[pl.BlockSpec((TM_MOE, d), tile),
                      pl.BlockSpec((1, d, D_EXPERT), lambda i, te, nt: (te[i], 0, 0)),
                      pl.BlockSpec((1, d, D_EXPERT), lambda i, te, nt: (te[i], 0, 0)),
                      pl.BlockSpec((1, D_EXPERT, d), lambda i, te, nt: (te[i], 0, 0))],
            out_specs=pl.BlockSpec((TM_MOE, d), lambda i, te, nt: (i, 0)),
            scratch_shapes=[pltpu.VMEM((d, D_EXPERT), BF16), pltpu.VMEM((d, D_EXPERT), BF16),
                            pltpu.VMEM((D_EXPERT, d), BF16)]),
        out_shape=jax.ShapeDtypeStruct((n_rows, d), F32),
        compiler_params=_cparams(("arbitrary",)),
        name="moe_experts",
    )(tile_expert, n_tiles, xs, wg, wu, wd)


def _combine_kernel(dlo_ref, dhi_ref, ys_hbm, xf_ref, w_ref, lg_ref, lb_ref, of_ref, ob_ref, a_ref, b_ref, sem):
    i = pl.program_id(0)
    n = pl.num_programs(0)

    def issue(tile, slot):
        base = tile * TM_CMB

        def body(g, c):
            for u in range(ROW_UNROLL):
                r = g * ROW_UNROLL + u
                t = base + r
                _row_copy(ys_hbm, dlo_ref[t], a_ref.at[slot], r, sem.at[0, slot]).start(priority=u % 2)
                _row_copy(ys_hbm, dhi_ref[t], b_ref.at[slot], r, sem.at[1, slot]).start(priority=(u + 1) % 2)
            return c

        lax.fori_loop(0, TM_CMB // ROW_UNROLL, body, 0)

    @pl.when(i == 0)
    def _():
        issue(0, 0)

    @pl.when(i + 1 < n)
    def _():
        issue(i + 1, (i + 1) % 2)

    slot = i % 2
    pltpu.make_async_copy(ys_hbm.at[pl.ds(0, TM_CMB)], a_ref.at[slot], sem.at[0, slot]).wait()
    pltpu.make_async_copy(ys_hbm.at[pl.ds(0, TM_CMB)], b_ref.at[slot], sem.at[1, slot]).wait()
    w = w_ref[...]
    ffn = w[:, 0:1] * a_ref[slot] + w[:, 1:2] * b_ref[slot]
    y = _layer_norm(ALPHA * xf_ref[...] + ffn, lg_ref[...], lb_ref[...])
    of_ref[...] = y
    ob_ref[...] = y.astype(BF16)


def _combine_call(dlo, dhi, ys, xf, w2, lg, lb):
    n, d = xf.shape
    row = lambda wdt: pl.BlockSpec((TM_CMB, wdt), lambda i, a, b: (i, 0))
    vec = pl.BlockSpec((1, d), lambda i, a, b: (0, 0))
    return pl.pallas_call(
        _combine_kernel,
        grid_spec=pltpu.PrefetchScalarGridSpec(
            num_scalar_prefetch=2,
            grid=(n // TM_CMB,),
            in_specs=[pl.BlockSpec(memory_space=pl.ANY), row(d), row(2), vec, vec],
            out_specs=[row(d), row(d)],
            scratch_shapes=[pltpu.VMEM((2, TM_CMB, d), F32), pltpu.VMEM((2, TM_CMB, d), F32),
                            pltpu.SemaphoreType.DMA((2, 2))]),
        out_shape=[jax.ShapeDtypeStruct((n, d), F32), jax.ShapeDtypeStruct((n, d), BF16)],
        compiler_params=_cparams(("arbitrary",)),
        name="moe_combine",
    )(dlo, dhi, ys, xf, w2, lg, lb)


def _moe(xf, wr_t, rb, wg, wu, wd, expert_base, lg, lb):
    n, d = xf.shape
    idx, wgt, cnt = _router_call(xf, wr_t, rb)
    counts = cnt[:, 0].astype(I32)
    padded = (counts + TM_MOE - 1) // TM_MOE * TM_MOE
    ends = jnp.cumsum(padded)
    starts = ends - padded
    max_tiles = (TOP_K * n) // TM_MOE + N_EXPERTS
    n_tiles = (ends[-1] // TM_MOE).astype(I32).reshape(1)
    tile_ids = jnp.arange(max_tiles, dtype=I32)
    tile_expert = jnp.minimum(jnp.sum((tile_ids[:, None] >= (ends // TM_MOE)[None, :]).astype(I32), axis=1),
                              N_EXPERTS - 1)
    onehot = lambda e: (e[None, :] == jnp.arange(N_EXPERTS, dtype=I32)[:, None])
    dlo = jnp.sum(jnp.where(onehot(idx[0]), starts[:, None], 0), axis=0) + idx[2]
    dhi = jnp.sum(jnp.where(onehot(idx[1]), starts[:, None], 0), axis=0) + idx[3]
    tail = n_tiles[0] + jnp.arange(N_EXPERTS, dtype=I32)
    pad_start = jnp.concatenate([jnp.where(counts % TM_MOE != 0, ends - TM_MOE, -1),
                                 jnp.where(tail < max_tiles, tail * TM_MOE, -1)]).astype(I32)
    xs = _dispatch_call(dlo, dhi, pad_start, xf, max_tiles * TM_MOE)
    ys = _experts_call(tile_expert + expert_base, n_tiles, xs, wg, wu, wd)
    return _combine_call(dlo, dhi, ys, xf, wgt[:2].T, lg, lb)


def _rope_tables(seq):
    pos = jnp.arange(seq, dtype=F32)[:, None]
    lane = np.arange(LANES)

    def angles(half):
        inv = ROPE_THETA ** (-jnp.arange(half, dtype=F32) / half)
        return pos * inv[None, :]

    def lane_table(half, lo, hi):
        ang = angles(half)
        idx = (lane - lo) % half
        active = (lane >= lo) & (lane < hi)
        first = ((lane - lo) % (2 * half)) < half
        cos = jnp.where(active[None, :], jnp.cos(ang)[:, idx], 1.0)
        sin = jnp.where(active[None, :], jnp.sin(ang)[:, idx] * np.where(first, -1.0, 1.0)[None, :], 0.0)
        return cos.astype(F32), sin.astype(F32)

    def feat_table(half):
        ang = angles(half)
        return jnp.cos(ang).T.astype(F32), jnp.sin(ang).T.astype(F32)

    c64, s64 = lane_table(B_HDIM // 2, 0, LANES)
    c64t, s64t = feat_table(B_HDIM // 2)
    ck, sk = lane_table(A_ROPE // 2, 0, A_ROPE)
    cqt, sqt = feat_table(A_ROPE // 2)
    return (c64, s64, c64t, s64t), (cqt, sqt, ck, sk)


def _layer_weights(l, w_in, a_w_uq, a_w_ukv):
    d = D_MODEL
    pts = [int(p) for p in np.cumsum(SPLIT_SIZES)[:-1]]
    (a_q, a_kv, a_kpe, b_q, b_k, b_v, b_qi, b_ki, b_wi, c_q, c_k, c_v, gates) = jnp.split(w_in[l], pts, axis=1)
    zeros = lambda n: jnp.zeros((d, n), F32)
    bf = lambda a: a.astype(BF16)
    w_mla = jnp.concatenate([a_q, a_kv, a_kpe, zeros(LANES - A_ROPE)], axis=1)
    dsa_w = jnp.concatenate([b_k, b_ki, zeros(LANES - IDX_DIM)], axis=1)
    qi_t = jnp.concatenate([b_qi.reshape(d, IDX_HEADS, IDX_DIM),
                            jnp.zeros((d, IDX_HEADS, LANES - IDX_DIM), F32)], axis=2).reshape(d, -1).T
    wi_t = jnp.concatenate([b_wi * (IDX_HEADS * IDX_DIM) ** -0.5, zeros(WI_ROWS - IDX_HEADS)], axis=1).T
    dsa_wt = jnp.concatenate([(b_q * (B_HDIM ** -0.5 * LOG2E)).T, b_v.T, qi_t, wi_t], axis=0)
    swa_wt = jnp.concatenate([(c_q * (C_HDIM ** -0.5 * LOG2E)).T, c_v.T], axis=0)
    uq = a_w_uq[l].reshape(A_Q_RANK, A_HEADS, A_NOPE + A_ROPE) * ((A_NOPE + A_ROPE) ** -0.5 * LOG2E)
    wq = jnp.concatenate([uq, jnp.zeros((A_Q_RANK, A_HEADS, LANES - A_NOPE - A_ROPE), F32)], axis=2)
    ukv = a_w_ukv[l].reshape(A_KV_RANK, A_HEADS, A_NOPE + A_VDIM)
    wk = jnp.concatenate([ukv[:, :, :A_NOPE], jnp.zeros((A_KV_RANK, A_HEADS, LANES - A_NOPE), F32)], axis=2)
    wv = ukv[:, :, A_NOPE:]
    return dict(w_mla=bf(w_mla), dsa_w=bf(dsa_w), dsa_wt=bf(dsa_wt), swa_w=bf(c_k), swa_wt=bf(swa_wt),
                w_gate=bf(gates), wqt=bf(wq.reshape(A_Q_RANK, -1).T), wk=bf(wk.reshape(A_KV_RANK, -1)),
                wvt=bf(wv.reshape(A_KV_RANK, -1).T))


QW = B_HEADS * B_HDIM
DSA_TOK = ((0, LANES, True), (LANES, LANES, True))
DSA_FEAT = ((0, QW, True, "wide"), (QW, LANES, False, TK), (QW + LANES, IDX_HEADS * LANES, True, 0),
            (QW + LANES + IDX_HEADS * LANES, WI_ROWS, False, 0))
SWA_TOK = ((0, LANES, True),)
SWA_FEAT = ((0, QW, True, "wide"), (QW, LANES, False, SWA_KB))


def kernel(x, ln_in_g, ln_in_b, w_in, a_q_ln_g, a_kv_ln_g, a_w_uq, a_w_ukv, c_sinks, w_br_a, w_br_b, w_br_c,
           w_out, ln1_g, ln1_b, w_router, router_bias, w_exp_gate, w_exp_up, w_exp_down, ln2_g, ln2_b):
    batch, seq, d = x.shape
    assert d == D_MODEL and seq % TM == 0 and seq % TQ == 0 and TQ == TK
    n = batch * seq
    tabs64, mla_tabs = _rope_tables(seq)
    xf, xb = _ln_call(x.reshape(n, d), ln_in_g, ln_in_b)
    wr_t = w_router.T
    rb = router_bias.reshape(N_EXPERTS, 1)
    wg_all = w_exp_gate.reshape(DEPTH * N_EXPERTS, d, D_EXPERT)
    wu_all = w_exp_up.reshape(DEPTH * N_EXPERTS, d, D_EXPERT)
    wd_all = w_exp_down.reshape(DEPTH * N_EXPERTS, D_EXPERT, d)
    bf = lambda a: a.astype(BF16)
    vec = lambda a: a.reshape(1, -1)
    for l in range(DEPTH):
        w = _layer_weights(l, w_in, a_w_uq, a_w_ukv)
        qa, ka, va = _mla_proj_call(xb, w["w_mla"], vec(a_q_ln_g[l]), vec(a_kv_ln_g[l]), w["wqt"], w["wk"],
                                    w["wvt"], mla_tabs, seq)
        o_a = _mla_attn_call(qa, ka, va, batch, seq)
        bk, bki, bq, bv, bqi, bwi = _qkv_proj_call(xb, w["dsa_w"], w["dsa_wt"], tabs64, DSA_TOK, (BF16, BF16),
                                                   DSA_FEAT, (BF16, BF16, BF16, F32), "dsa_proj", seq)
        o_b = _dsa_attn_call(bq, bk, bv, bqi, bki, bwi, batch, seq)
        ck_, cq_, cv_ = _qkv_proj_call(xb, w["swa_w"], w["swa_wt"], tabs64, SWA_TOK, (BF16,),
                                       SWA_FEAT, (BF16, BF16), "swa_proj", seq)
        o_c = _swa_attn_call(c_sinks[l], cq_, ck_, cv_, batch, seq)
        g0, g1, g2 = _gate_proj_call(xb, w["w_gate"])
        xf, xb = _merge_call(o_a, o_b, o_c, g0, g1, g2, xf, bf(w_br_a[l]), bf(w_br_b[l]), bf(w_br_c[l]),
                             bf(w_out[l]), vec(ln1_g[l]), vec(ln1_b[l]))
        xf, xb = _moe(xf, wr_t, rb, wg_all, wu_all, wd_all, l * N_EXPERTS, vec(ln2_g[l]), vec(ln2_b[l]))
    return xf.reshape(batch, seq, d)
```

```python
import functools

import jax
import jax.numpy as jnp
import numpy as np
from jax import lax
from jax.experimental import pallas as pl
from jax.experimental.pallas import tpu as pltpu

F32 = jnp.float32
BF16 = jnp.bfloat16
I32 = jnp.int32

D_MODEL = 1024
DEPTH = 2
CHUNK = 64
ROPE_THETA = 10000.0
LN_EPS = 1e-5
RMS_EPS = 1e-6
NEG = -1e30

A_HEADS, A_NOPE, A_ROPE, A_VDIM, A_Q_RANK, A_KV_RANK = 8, 64, 32, 64, 384, 256
B_HEADS, B_KV_HEADS, B_HDIM, IDX_HEADS, IDX_DIM, TOPK_MAX = 8, 2, 64, 4, 64, 256
C_HEADS, C_KV_HEADS, C_HDIM, WINDOW = 8, 2, 64, 128
WIN_CHUNKS = WINDOW // CHUNK
N_EXPERTS, N_GROUPS, TOP_K, D_EXPERT = 16, 4, 2, 512
EXPERTS_PER_GROUP = N_EXPERTS // N_GROUPS
ALPHA = (2.0 * DEPTH) ** 0.25
LOG2E = 1.4426950408889634

SPLIT_SIZES = (
    A_Q_RANK, A_KV_RANK, A_ROPE,
    B_HEADS * B_HDIM, B_KV_HEADS * B_HDIM, B_KV_HEADS * B_HDIM,
    IDX_HEADS * IDX_DIM, IDX_DIM, IDX_HEADS,
    C_HEADS * C_HDIM, C_KV_HEADS * C_HDIM, C_KV_HEADS * C_HDIM,
    3 * D_MODEL,
)

LANES = 128
SUBLANES = 8
TM = 512
TQ = 256
TK = 256
SWA_KB = WIN_CHUNKS * CHUNK
TM_MOE = 512
TM_CMB = 256
ROW_UNROLL = 8
WI_ROWS = 16
VMEM_LIMIT = 56 * 1024 * 1024
F32_TINY = 1.1754943508222875e-38


def _cparams(sem):
    return pltpu.CompilerParams(dimension_semantics=sem, vmem_limit_bytes=VMEM_LIMIT)


def _dot(a, b):
    return jnp.dot(a, b, preferred_element_type=F32)


def _dot_nt(a, b):
    return lax.dot_general(a, b, (((1,), (1,)), ((), ())), preferred_element_type=F32)


def _layer_norm(z, g, b):
    mu = jnp.mean(z, axis=-1, keepdims=True)
    zc = z - mu
    var = jnp.mean(zc * zc, axis=-1, keepdims=True)
    return zc * lax.rsqrt(var + LN_EPS) * g + b


def _ln_kernel(x_ref, g_ref, b_ref, of_ref, ob_ref):
    y = _layer_norm(x_ref[...], g_ref[...], b_ref[...])
    of_ref[...] = y
    ob_ref[...] = y.astype(BF16)


def _ln_call(x, g, b):
    n, d = x.shape
    row = pl.BlockSpec((TM, d), lambda i: (i, 0))
    vec = pl.BlockSpec((1, d), lambda i: (0, 0))
    return pl.pallas_call(
        _ln_kernel,
        grid=(n // TM,),
        in_specs=[row, vec, vec],
        out_specs=[row, row],
        out_shape=[jax.ShapeDtypeStruct((n, d), F32), jax.ShapeDtypeStruct((n, d), BF16)],
        compiler_params=_cparams(("parallel",)),
        name="ln_in",
    )(x, g.reshape(1, d), b.reshape(1, d))


def _rope_lanes(y, cos, sin, lane, half):
    sw = jnp.where((lane & (2 * half - 1)) < half, pltpu.roll(y, LANES - half, 1), pltpu.roll(y, half, 1))
    return y * cos + sw * sin


def _qkv_proj_kernel(tok_segs, feat_segs, x_ref, w_ref, wt_ref, cos_ref, sin_ref, cost_ref, sint_ref, *out_refs):
    x = x_ref[...]
    n_tok = len(tok_segs)
    cos = cos_ref[...]
    sin = sin_ref[...]
    lane = lax.broadcasted_iota(I32, cos.shape, 1)
    for (start, width, rope), o_ref in zip(tok_segs, out_refs[:n_tok]):
        y = _dot(x, w_ref[:, start:start + width])
        for c in range(width // LANES):
            yc = y[:, c * LANES:(c + 1) * LANES]
            if rope:
                yc = _rope_lanes(yc, cos, sin, lane, B_HDIM // 2)
            o_ref[:, c * LANES:(c + 1) * LANES] = yc.astype(o_ref.dtype)
    ct = cost_ref[...]
    st = sint_ref[...]
    half = B_HDIM // 2
    for (start, rows, rope, kb), o_ref in zip(feat_segs, out_refs[n_tok:]):
        yt = _dot_nt(wt_ref[start:start + rows, :], x)
        if kb == "wide":
            n_heads = rows // B_HDIM
            rep = n_heads // (LANES // B_HDIM)
            zero = jnp.zeros((B_HDIM, TQ), o_ref.dtype)
            for h in range(n_heads):
                x0 = yt[h * B_HDIM:h * B_HDIM + half]
                x1 = yt[h * B_HDIM + half:(h + 1) * B_HDIM]
                r0 = (x0 * ct - x1 * st).astype(o_ref.dtype)
                r1 = (x1 * ct + x0 * st).astype(o_ref.dtype)
                g = h // rep
                for c in range(yt.shape[1] // TQ):
                    cs = slice(c * TQ, (c + 1) * TQ)
                    hs = slice(h * TQ, (h + 1) * TQ)
                    o_ref[c, g * B_HDIM:g * B_HDIM + half, hs] = r0[:, cs]
                    o_ref[c, g * B_HDIM + half:(g + 1) * B_HDIM, hs] = r1[:, cs]
                    o_ref[c, (1 - g) * B_HDIM:(2 - g) * B_HDIM, hs] = zero
        elif rope:
            for r in range(rows // B_HDIM):
                x0 = yt[r * B_HDIM:r * B_HDIM + half]
                x1 = yt[r * B_HDIM + half:(r + 1) * B_HDIM]
                o_ref[r * B_HDIM:r * B_HDIM + half, :] = (x0 * ct - x1 * st).astype(o_ref.dtype)
                o_ref[r * B_HDIM + half:(r + 1) * B_HDIM, :] = (x1 * ct + x0 * st).astype(o_ref.dtype)
        elif kb:
            for c in range(yt.shape[1] // kb):
                o_ref[c] = yt[:, c * kb:(c + 1) * kb].astype(o_ref.dtype)
        else:
            o_ref[...] = yt.astype(o_ref.dtype)


def _qkv_proj_call(xb, w, wt, tabs, tok_segs, tok_dtypes, feat_segs, feat_dtypes, name, seq):
    n, d = xb.shape
    tpb = seq // TM
    cos, sin, cost, sint = tabs
    full = lambda a: pl.BlockSpec(a.shape, lambda i: (0, 0))
    out_specs, out_shape = [], []
    for (start, width, rope), dt in zip(tok_segs, tok_dtypes):
        out_specs.append(pl.BlockSpec((TM, width), lambda i: (i, 0)))
        out_shape.append(jax.ShapeDtypeStruct((n, width), dt))
    for (start, rows, rope, kb), dt in zip(feat_segs, feat_dtypes):
        if kb == "wide":
            wide = (rows // B_HDIM) * TQ
            out_specs.append(pl.BlockSpec((TM // TQ, LANES, wide), lambda i: (i, 0, 0)))
            out_shape.append(jax.ShapeDtypeStruct((n // TQ, LANES, wide), dt))
        elif kb:
            out_specs.append(pl.BlockSpec((TM // kb, rows, kb), lambda i: (i, 0, 0)))
            out_shape.append(jax.ShapeDtypeStruct((n // kb, rows, kb), dt))
        else:
            out_specs.append(pl.BlockSpec((rows, TM), lambda i: (0, i)))
            out_shape.append(jax.ShapeDtypeStruct((rows, n), dt))
    return pl.pallas_call(
        functools.partial(_qkv_proj_kernel, tok_segs, feat_segs),
        grid=(n // TM,),
        in_specs=[pl.BlockSpec((TM, d), lambda i: (i, 0)), full(w), full(wt),
                  pl.BlockSpec((TM, LANES), lambda i: (i % tpb, 0)),
                  pl.BlockSpec((TM, LANES), lambda i: (i % tpb, 0)),
                  pl.BlockSpec((B_HDIM // 2, TM), lambda i: (0, i % tpb)),
                  pl.BlockSpec((B_HDIM // 2, TM), lambda i: (0, i % tpb))],
        out_specs=out_specs,
        out_shape=out_shape,
        compiler_params=_cparams(("parallel",)),
        name=name,
    )(xb, w, wt, cos, sin, cost, sint)


def _gate_proj_kernel(x_ref, w_ref, *out_refs):
    x = x_ref[...]
    for c, o_ref in enumerate(out_refs):
        y = _dot(x, w_ref[:, c * D_MODEL:(c + 1) * D_MODEL])
        o_ref[...] = jax.nn.sigmoid(y).astype(o_ref.dtype)


def _gate_proj_call(xb, w):
    n, d = xb.shape
    row = pl.BlockSpec((TM, d), lambda i: (i, 0))
    return pl.pallas_call(
        _gate_proj_kernel,
        grid=(n // TM,),
        in_specs=[row, pl.BlockSpec(w.shape, lambda i: (0, 0))],
        out_specs=[row] * 3,
        out_shape=[jax.ShapeDtypeStruct((n, d), BF16)] * 3,
        compiler_params=_cparams(("parallel",)),
        name="gate_proj",
    )(xb, w)


def _mla_proj_kernel(x_ref, w1_ref, qg_ref, kvg_ref, wqt_ref, wk_ref, wvt_ref,
                     cqt_ref, sqt_ref, ck_ref, sk_ref, qt_ref, k_ref, vt_ref):
    y = _dot(x_ref[...], w1_ref[...])
    ql = y[:, :A_Q_RANK]
    kvl = y[:, A_Q_RANK:A_Q_RANK + A_KV_RANK]
    kpe = y[:, A_Q_RANK + A_KV_RANK:]
    qn = (ql * lax.rsqrt(jnp.mean(ql * ql, axis=-1, keepdims=True) + RMS_EPS) * qg_ref[...]).astype(BF16)
    kvn = (kvl * lax.rsqrt(jnp.mean(kvl * kvl, axis=-1, keepdims=True) + RMS_EPS) * kvg_ref[...]).astype(BF16)
    half = A_ROPE // 2
    qt = _dot_nt(wqt_ref[...], qn)
    ct = cqt_ref[...]
    st = sqt_ref[...]
    for h in range(A_HEADS):
        base = h * LANES
        r0, r1, r2 = base + A_NOPE, base + A_NOPE + half, base + A_NOPE + A_ROPE
        x0 = qt[r0:r1]
        x1 = qt[r1:r2]
        qt_ref[base:r0, :] = qt[base:r0].astype(BF16)
        qt_ref[r0:r1, :] = (x0 * ct - x1 * st).astype(BF16)
        qt_ref[r1:r2, :] = (x1 * ct + x0 * st).astype(BF16)
        qt_ref[r2:base + LANES, :] = qt[r2:base + LANES].astype(BF16)
    kn = _dot(kvn, wk_ref[...])
    lane = lax.broadcasted_iota(I32, kpe.shape, 1)
    kpe_sw = jnp.where(lane < half, pltpu.roll(kpe, LANES - half, 1), pltpu.roll(kpe, half, 1))
    kpe_p = pltpu.roll(kpe * ck_ref[...] + kpe_sw * sk_ref[...], A_NOPE, 1)
    for h in range(A_HEADS):
        sl = slice(h * LANES, (h + 1) * LANES)
        k_ref[:, sl] = (kn[:, sl] + kpe_p).astype(BF16)
    vt = _dot_nt(wvt_ref[...], kvn)
    for c in range(vt.shape[1] // TK):
        vt_ref[c] = vt[:, c * TK:(c + 1) * TK].astype(BF16)


def _mla_proj_call(xb, w1, qg, kvg, wqt, wk, wvt, tabs, seq):
    n, d = xb.shape
    tpb = seq // TM
    cqt, sqt, ck, sk = tabs
    full = lambda a: pl.BlockSpec(a.shape, lambda i: (0, 0))
    ttab = pl.BlockSpec((A_ROPE // 2, TM), lambda i: (0, i % tpb))
    ltab = pl.BlockSpec((TM, LANES), lambda i: (i % tpb, 0))
    hw = A_HEADS * LANES
    vw = A_HEADS * A_VDIM
    return pl.pallas_call(
        _mla_proj_kernel,
        grid=(n // TM,),
        in_specs=[pl.BlockSpec((TM, d), lambda i: (i, 0)), full(w1), full(qg), full(kvg), full(wqt), full(wk),
                  full(wvt), ttab, ttab, ltab, ltab],
        out_specs=[pl.BlockSpec((hw, TM), lambda i: (0, i)),
                   pl.BlockSpec((TM, hw), lambda i: (i, 0)),
                   pl.BlockSpec((TM // TK, vw, TK), lambda i: (i, 0, 0))],
        out_shape=[jax.ShapeDtypeStruct((hw, n), BF16), jax.ShapeDtypeStruct((n, hw), BF16),
                   jax.ShapeDtypeStruct((n // TK, vw, TK), BF16)],
        compiler_params=_cparams(("parallel",)),
        name="mla_proj",
    )(xb, w1, qg, kvg, wqt, wk, wvt, cqt, sqt, ck, sk)


def _chunk_of(pos):
    return lax.shift_right_logical(pos, 6)


def _diag_visible():
    k = lax.broadcasted_iota(I32, (TK, TQ), 0)
    q = lax.broadcasted_iota(I32, (TK, TQ), 1)
    return _chunk_of(k) <= _chunk_of(q)


def _online_step(h, s, vt, m_ref, l_ref, acc_ref):
    m_prev = m_ref[h]
    m_new = jnp.maximum(m_prev, jnp.max(s, axis=0, keepdims=True))
    alpha = jnp.exp2(m_prev - m_new)
    p = jnp.exp2(s - m_new)
    l_ref[h] = alpha * l_ref[h] + jnp.sum(p, axis=0, keepdims=True)
    acc_ref[h] = alpha * acc_ref[h] + _dot(vt, p.astype(BF16))
    m_ref[h] = m_new


def _init_state(m_ref, l_ref, acc_ref):
    m_ref[...] = jnp.full(m_ref.shape, NEG, F32)
    l_ref[...] = jnp.zeros(l_ref.shape, F32)
    acc_ref[...] = jnp.zeros(acc_ref.shape, F32)


def _write_heads(o_ref, l_ref, acc_ref, n_heads):
    for j in range(n_heads // 2):
        ot = jnp.concatenate([acc_ref[2 * j] / l_ref[2 * j], acc_ref[2 * j + 1] / l_ref[2 * j + 1]], axis=0)
        o_ref[:, j * LANES:(j + 1) * LANES] = ot.T.astype(o_ref.dtype)


def _mla_attn_kernel(qt_ref, k_ref, vt_ref, o_ref, m_ref, l_ref, acc_ref):
    i = pl.program_id(1)
    _init_state(m_ref, l_ref, acc_ref)
    diag = _diag_visible()

    def block(j, masked):
        rows = pl.ds(pl.multiple_of(j * TK, TK), TK)
        vt = vt_ref[j]
        ss = [_dot(k_ref[rows, h * LANES:(h + 1) * LANES], qt_ref[h * LANES:(h + 1) * LANES, :])
              for h in range(A_HEADS)]
        for h in range(A_HEADS):
            s = jnp.where(diag, ss[h], NEG) if masked else ss[h]
            _online_step(h, s, vt[h * A_VDIM:(h + 1) * A_VDIM], m_ref, l_ref, acc_ref)

    def body(j, c):
        block(j, False)
        return c

    lax.fori_loop(0, i, body, 0)
    block(i, True)
    _write_heads(o_ref, l_ref, acc_ref, A_HEADS)


def _attn_scratch(n_heads, dv):
    return [pltpu.VMEM((n_heads, 1, TQ), F32), pltpu.VMEM((n_heads, 1, TQ), F32),
            pltpu.VMEM((n_heads, dv, TQ), F32)]


def _mla_attn_call(qt, k, vt, batch, seq):
    nq = seq // TQ
    nkb = seq // TK
    hw = A_HEADS * LANES
    vw = A_HEADS * A_VDIM
    return pl.pallas_call(
        _mla_attn_kernel,
        grid=(batch, nq),
        in_specs=[pl.BlockSpec((hw, TQ), lambda b, i: (0, b * nq + i)),
                  pl.BlockSpec((seq, hw), lambda b, i: (b, 0)),
                  pl.BlockSpec((nkb, vw, TK), lambda b, i: (b, 0, 0))],
        out_specs=pl.BlockSpec((TQ, vw), lambda b, i: (b * nq + i, 0)),
        out_shape=jax.ShapeDtypeStruct((batch * seq, vw), BF16),
        scratch_shapes=_attn_scratch(A_HEADS, A_VDIM),
        compiler_params=_cparams(("parallel", "arbitrary")),
        name="mla_attn",
    )(qt, k, vt)


def _dsa_attn_kernel(n_sel, idx_bits, qw_ref, k_ref, vt_ref, qit_ref, ki_ref, wt_ref, o_ref,
                     key_ref, part_ref, tie_ref, bias_ref, j_ref, m_ref, l_ref, acc_ref):
    i = pl.program_id(1)
    nblk = i + 1
    diag = _diag_visible()
    kpos = lax.broadcasted_iota(I32, (TK, TQ), 0)
    w = wt_ref[...]

    def score_block(jb, c):
        ki = ki_ref[pl.ds(pl.multiple_of(jb * TK, TK), TK), :]
        sc = jnp.zeros((TK, TQ), F32)
        for h in range(IDX_HEADS):
            rel = jnp.maximum(_dot(ki, qit_ref[h * LANES:(h + 1) * LANES, :]), 0.0)
            sc = sc + w[h:h + 1, :] * rel
        sc = jnp.where(jnp.logical_or(jb < i, diag), sc, NEG)
        sc = jnp.where(jnp.abs(sc) < F32_TINY, 0.0, sc)
        bits = lax.bitcast_convert_type(sc, I32)
        key_ref[jb] = jnp.where(bits < 0, bits ^ 0x7FFFFFFF, bits)
        part_ref[jb] = lax.bitcast_convert_type(bits & -65536, F32).astype(BF16)
        return c

    lax.fori_loop(0, nblk, score_block, 0)

    one16 = jnp.ones((TK, TQ), BF16)
    zero16 = jnp.zeros((TK, TQ), BF16)
    rows16 = 2 * SUBLANES

    def count16(pred):
        def body(jb, acc):
            ind = jnp.where(pred(part_ref[jb]), one16, zero16).reshape(TK // rows16, rows16, TQ)
            parts = [ind[r] for r in range(TK // rows16)]
            while len(parts) > 1:
                parts = [parts[a] + parts[a + 1] for a in range(0, len(parts), 2)]
            return acc + parts[0]
        acc = lax.fori_loop(0, nblk, body, jnp.zeros((rows16, TQ), BF16))
        return jnp.sum(acc.astype(F32), axis=0, keepdims=True).astype(I32)

    def greedy(n_bits, start, to_float, want):
        def step(s, v):
            cand = v | lax.shift_left(jnp.int32(1), n_bits - 1 - s)
            cf = to_float(cand)
            c = count16(lambda x: x >= cf)
            return jnp.where(c >= want, cand, v)
        return lax.fori_loop(0, n_bits, step, start)

    def upper_float(kh):
        p = jnp.where(kh >= 0, kh, kh ^ 0x7FFF)
        p = jnp.where(jnp.logical_and(p > 0, p < 0x80), 0x80, p)
        return lax.bitcast_convert_type(lax.shift_left(p, 16), F32).astype(BF16)

    small_float = lambda v: v.astype(F32).astype(BF16)

    c0 = count16(lambda x: x >= jnp.zeros((1, TQ), BF16))
    th = greedy(15, jnp.where(c0 >= n_sel, 0, -32768).astype(I32), upper_float, n_sel)
    thf = upper_float(th)
    need_m = n_sel - count16(lambda x: x > thf)

    def mid_block(jb, c):
        k = key_ref[jb]
        band = lax.shift_right_arithmetic(k, 16) == th
        mid = (lax.shift_right_logical(k, 8) & 0xFF).astype(F32)
        part_ref[jb] = jnp.where(band, mid, -1.0).astype(BF16)
        return c

    lax.fori_loop(0, nblk, mid_block, 0)
    tm = greedy(8, jnp.zeros((1, TQ), I32), small_float, need_m)
    tmf = small_float(tm)
    need_l = need_m - count16(lambda x: x > tmf)

    t24 = lax.shift_left(th, 8) | tm

    def low_block(jb, c):
        k = key_ref[jb]
        band = lax.shift_right_arithmetic(k, 8) == t24
        part_ref[jb] = jnp.where(band, (k & 0xFF).astype(F32), -1.0).astype(BF16)
        return c

    lax.fori_loop(0, nblk, low_block, 0)
    tl = greedy(8, jnp.zeros((1, TQ), I32), small_float, need_l)
    tlf = small_float(tl)
    t = lax.shift_left(t24, 8) | tl
    need = need_l - count16(lambda x: x > tlf)
    c_eq = count16(lambda x: x == tlf)

    j_ref[...] = jnp.full(j_ref.shape, 2 ** 30, I32)

    @pl.when(jnp.max(c_eq - need) > 0)
    def _():
        def tie_block(jb, c):
            tie_ref[jb] = jnp.where(key_ref[jb] == t, jb * TK + kpos, 2 ** 30)
            return c

        lax.fori_loop(0, nblk, tie_block, 0)

        def count_below(cand):
            def body(jb, acc):
                c = (tie_ref[jb] < cand).astype(I32)
                return acc + jnp.sum(c.reshape(TK // SUBLANES, SUBLANES, TQ), axis=0)
            acc = lax.fori_loop(0, nblk, body, jnp.zeros((SUBLANES, TQ), I32))
            return jnp.sum(acc, axis=0, keepdims=True)

        def index_step(s, jv):
            cand = jv | lax.shift_left(jnp.int32(1), idx_bits - 1 - s)
            return jnp.where(count_below(cand) < need, cand, jv)

        j_ref[...] = lax.fori_loop(0, idx_bits, index_step, jnp.zeros(j_ref.shape, I32))

    jv = j_ref[...]

    def bias_block(jb, c):
        k = key_ref[jb]
        g = jb * TK + kpos
        sel = jnp.logical_or(k > t, jnp.logical_and(k == t, g <= jv))
        vis = jnp.logical_or(jb < i, diag)
        bias_ref[jb] = jnp.where(jnp.logical_and(sel, vis), 0.0, NEG)
        return c

    lax.fori_loop(0, nblk, bias_block, 0)

    _init_state(m_ref, l_ref, acc_ref)

    def attn_block(jb, c):
        kb = k_ref[pl.ds(pl.multiple_of(jb * TK, TK), TK), :]
        vt = vt_ref[jb]
        bias = bias_ref[jb]
        s = _dot(kb, qw_ref[0])
        s = jnp.concatenate([s[:, h * TQ:(h + 1) * TQ] + bias for h in range(B_HEADS)], axis=1)
        m_prev = m_ref[...]
        m_new = jnp.maximum(m_prev, jnp.max(s, axis=0, keepdims=True))
        alpha = jnp.exp2(m_prev - m_new)
        p = jnp.exp2(s - m_new)
        l_ref[...] = alpha * l_ref[...] + jnp.sum(p, axis=0, keepdims=True)
        acc_ref[...] = alpha * acc_ref[...] + _gqa_pv([vt], p.astype(BF16), B_KV_HEADS, B_HDIM)
        m_ref[...] = m_new
        return c

    lax.fori_loop(0, nblk, attn_block, 0)
    _write_wide(o_ref, acc_ref[...] / l_ref[...], B_HEADS)


def _gqa_pv(vts, pb, n_groups, hdim):
    gw = pb.shape[1] // n_groups
    outs = []
    for g in range(n_groups):
        o = None
        k0 = 0
        for vt in vts:
            kc = vt.shape[1]
            part = _dot(vt[g * hdim:(g + 1) * hdim], pb[k0:k0 + kc, g * gw:(g + 1) * gw])
            o = part if o is None else o + part
            k0 += kc
        outs.append(o)
    return jnp.concatenate(outs, axis=1)


def _write_wide(o_ref, ow, n_heads):
    for j in range(n_heads // 2):
        ot = jnp.concatenate([ow[:, 2 * j * TQ:(2 * j + 1) * TQ], ow[:, (2 * j + 1) * TQ:(2 * j + 2) * TQ]], axis=0)
        o_ref[:, j * LANES:(j + 1) * LANES] = ot.T.astype(o_ref.dtype)


def _wide_scratch(n_heads, dv):
    return [pltpu.VMEM((1, n_heads * TQ), F32), pltpu.VMEM((1, n_heads * TQ), F32),
            pltpu.VMEM((dv, n_heads * TQ), F32)]


def _dsa_attn_call(qw, k, vt, qit, ki, wt, batch, seq):
    nq = seq // TQ
    nkb = seq // TK
    n_sel = min(TOPK_MAX, seq // 4)
    qcol = lambda rows: pl.BlockSpec((rows, TQ), lambda b, i: (0, b * nq + i))
    brow = lambda wdt: pl.BlockSpec((seq, wdt), lambda b, i: (b, 0))
    ow = B_HEADS * B_HDIM
    return pl.pallas_call(
        functools.partial(_dsa_attn_kernel, n_sel, max(1, (seq - 1).bit_length())),
        grid=(batch, nq),
        in_specs=[pl.BlockSpec((1,) + qw.shape[1:], lambda b, i: (b * nq + i, 0, 0)), brow(k.shape[1]),
                  pl.BlockSpec((nkb,) + vt.shape[1:], lambda b, i: (b, 0, 0)),
                  qcol(qit.shape[0]), brow(ki.shape[1]), qcol(wt.shape[0])],
        out_specs=pl.BlockSpec((TQ, ow), lambda b, i: (b * nq + i, 0)),
        out_shape=jax.ShapeDtypeStruct((batch * seq, ow), BF16),
        scratch_shapes=[pltpu.VMEM((nkb, TK, TQ), I32), pltpu.VMEM((nkb, TK, TQ), BF16),
                        pltpu.VMEM((nkb, TK, TQ), I32), pltpu.VMEM((nkb, TK, TQ), F32),
                        pltpu.VMEM((1, TQ), I32)] + _wide_scratch(B_HEADS, B_HDIM),
        compiler_params=_cparams(("parallel", "arbitrary")),
        name="dsa_attn",
    )(qw, k, vt, qit, ki, wt)


SWA_SLAB = TQ + WIN_CHUNKS * CHUNK


def _swa_attn_kernel(sink_ref, qw_ref, k_ref, vt_ref, o_ref):
    i = pl.program_id(1)
    s0 = pl.multiple_of(jnp.maximum(i * TQ - SWA_KB, 0), SWA_KB)
    kslab = k_ref[pl.ds(s0, SWA_SLAB), :]
    kc = _chunk_of(s0 + lax.broadcasted_iota(I32, (SWA_SLAB, TQ), 0))
    qc = _chunk_of(i * TQ + lax.broadcasted_iota(I32, (SWA_SLAB, TQ), 1))
    valid = jnp.logical_and(kc <= qc, kc >= qc - WIN_CHUNKS)
    b0 = lax.div(s0, SWA_KB)
    vts = [vt_ref[b0 + c] for c in range(SWA_SLAB // SWA_KB)]
    s = _dot(kslab, qw_ref[0])
    s = jnp.concatenate([jnp.where(valid, s[:, h * TQ:(h + 1) * TQ], NEG) for h in range(C_HEADS)], axis=1)
    sink = jnp.concatenate([jnp.full((1, TQ), sink_ref[h] * LOG2E, F32) for h in range(C_HEADS)], axis=1)
    m = jnp.maximum(jnp.max(s, axis=0, keepdims=True), sink)
    p = jnp.exp2(s - m)
    l = jnp.sum(p, axis=0, keepdims=True) + jnp.exp2(sink - m)
    _write_wide(o_ref, _gqa_pv(vts, p.astype(BF16), C_KV_HEADS, C_HDIM) / l, C_HEADS)


def _swa_attn_call(sinks, qw, k, vt, batch, seq):
    nq = seq // TQ
    nkb = seq // SWA_KB
    ow = C_HEADS * C_HDIM
    return pl.pallas_call(
        _swa_attn_kernel,
        grid=(batch, nq),
        in_specs=[pl.BlockSpec(memory_space=pltpu.SMEM),
                  pl.BlockSpec((1,) + qw.shape[1:], lambda b, i: (b * nq + i, 0, 0)),
                  pl.BlockSpec((seq, k.shape[1]), lambda b, i: (b, 0)),
                  pl.BlockSpec((nkb,) + vt.shape[1:], lambda b, i: (b, 0, 0))],
        out_specs=pl.BlockSpec((TQ, ow), lambda b, i: (b * nq + i, 0)),
        out_shape=jax.ShapeDtypeStruct((batch * seq, ow), BF16),
        compiler_params=_cparams(("parallel", "arbitrary")),
        name="swa_attn",
    )(sinks, qw, k, vt)


def _merge_kernel(oa_ref, ob_ref, oc_ref, ga_ref, gb_ref, gc_ref, x_ref, wa_ref, wb_ref, wc_ref, wo_ref,
                  lg_ref, lb_ref, of_ref, ob16_ref):
    merged = (ga_ref[...].astype(F32) * _dot(oa_ref[...], wa_ref[...])
              + gb_ref[...].astype(F32) * _dot(ob_ref[...], wb_ref[...])
              + gc_ref[...].astype(F32) * _dot(oc_ref[...], wc_ref[...]))
    mix = _dot(merged.astype(BF16), wo_ref[...])
    y = _layer_norm(ALPHA * x_ref[...] + mix, lg_ref[...], lb_ref[...])
    of_ref[...] = y
    ob16_ref[...] = y.astype(BF16)


def _merge_call(oa, ob, oc, ga, gb, gc, x, wa, wb, wc, wo, lg, lb):
    n, d = x.shape
    row = lambda wdt: pl.BlockSpec((TM, wdt), lambda i: (i, 0))
    full = lambda a: pl.BlockSpec(a.shape, lambda i: (0, 0))
    return pl.pallas_call(
        _merge_kernel,
        grid=(n // TM,),
        in_specs=[row(oa.shape[1]), row(ob.shape[1]), row(oc.shape[1]), row(d), row(d), row(d), row(d),
                  full(wa), full(wb), full(wc), full(wo), full(lg), full(lb)],
        out_specs=[row(d), row(d)],
        out_shape=[jax.ShapeDtypeStruct((n, d), F32), jax.ShapeDtypeStruct((n, d), BF16)],
        compiler_params=_cparams(("parallel",)),
        name="merge",
    )(oa, ob, oc, ga, gb, gc, x, wa, wb, wc, wo, lg, lb)


def _router_kernel(x_ref, wr_ref, rb_ref, idx_ref, wgt_ref, cnt_ref, mask_ref, carry_ref):
    @pl.when(pl.program_id(0) == 0)
    def _():
        carry_ref[...] = jnp.zeros(carry_ref.shape, F32)

    logits =lax.dot_general(wr_ref[...], x_ref[...], (((1,), (1,)), ((), ())),
                             precision=lax.Precision.HIGHEST, preferred_element_type=F32)
    scores = jax.nn.sigmoid(logits)
    biased = scores + rb_ref[...]
    sc = [scores[e:e + 1, :] for e in range(N_EXPERTS)]
    bi = [biased[e:e + 1, :] for e in range(N_EXPERTS)]
    epg = EXPERTS_PER_GROUP
    gs = []
    for g in range(N_GROUPS):
        b = bi[g * epg:(g + 1) * epg]
        best = None
        for a in range(epg):
            for c in range(a + 1, epg):
                pair = b[a] + b[c]
                best = pair if best is None else jnp.maximum(best, pair)
        gs.append(best)
    gsel = []
    for g in range(N_GROUPS):
        ok = None
        for o in range(N_GROUPS):
            if o == g:
                continue
            cond = gs[g] > gs[o] if o < g else gs[g] >= gs[o]
            ok = cond if ok is None else jnp.logical_and(ok, cond)
        gsel.append(ok)
    chosen = []
    for e in range(N_EXPERTS):
        g = e // epg
        rank = jnp.zeros_like(bi[e], dtype=I32)
        for o in range(g * epg, (g + 1) * epg):
            if o == e:
                continue
            ahead = bi[o] >= bi[e] if o < e else bi[o] > bi[e]
            rank = rank + ahead.astype(I32)
        chosen.append(jnp.logical_and(gsel[g], rank < TOP_K))
    picked = [jnp.where(chosen[e], sc[e], 0.0) for e in range(N_EXPERTS)]
    total = picked[0]
    for e in range(1, N_EXPERTS):
        total = total + picked[e]
    for e in range(N_EXPERTS):
        mask_ref[e:e + 1, :] = chosen[e].astype(F32)
    mask = mask_ref[...]
    tm = mask.shape[1]
    upper = (lax.broadcasted_iota(I32, (tm, tm), 0) <= lax.broadcasted_iota(I32, (tm, tm), 1)).astype(BF16)
    pos = carry_ref[...] + _dot(mask.astype(BF16), upper) - 1.0
    carry_ref[...] += jnp.sum(mask, axis=1, keepdims=True)
    cnt_ref[...] = jnp.broadcast_to(carry_ref[...], cnt_ref.shape)
    zero_i = jnp.zeros_like(bi[0], dtype=I32)
    zero_f = jnp.zeros_like(bi[0])
    seen = None
    e_lo, e_hi, p_lo, p_hi, w_lo, w_hi = zero_i, zero_i, zero_f, zero_f, zero_f, zero_f
    for e in range(N_EXPERTS):
        is_lo = chosen[e] if seen is None else jnp.logical_and(chosen[e], jnp.logical_not(seen))
        is_hi = jnp.zeros_like(chosen[e]) if seen is None else jnp.logical_and(chosen[e], seen)
        seen = chosen[e] if seen is None else jnp.logical_or(seen, chosen[e])
        pe = pos[e:e + 1, :]
        we = picked[e] / total
        e_lo = jnp.where(is_lo, e, e_lo)
        e_hi = jnp.where(is_hi, e, e_hi)
        p_lo = jnp.where(is_lo, pe, p_lo)
        p_hi = jnp.where(is_hi, pe, p_hi)
        w_lo = jnp.where(is_lo, we, w_lo)
        w_hi = jnp.where(is_hi, we, w_hi)
    idx_ref[...] = jnp.zeros(idx_ref.shape, I32)
    idx_ref[0:1, :] = e_lo
    idx_ref[1:2, :] = e_hi
    idx_ref[2:3, :] = p_lo.astype(I32)
    idx_ref[3:4, :] = p_hi.astype(I32)
    wgt_ref[...] = jnp.zeros(wgt_ref.shape, F32)
    wgt_ref[0:1, :] = w_lo
    wgt_ref[1:2, :] = w_hi


def _router_call(x, wr_t, rb):
    n, d = x.shape
    return pl.pallas_call(
        _router_kernel,
        grid=(n // TM,),
        in_specs=[pl.BlockSpec((TM, d), lambda i: (i, 0)),
                  pl.BlockSpec((N_EXPERTS, d), lambda i: (0, 0)),
                  pl.BlockSpec((N_EXPERTS, 1), lambda i: (0, 0))],
        out_specs=[pl.BlockSpec((SUBLANES, TM), lambda i: (0, i)),
                   pl.BlockSpec((SUBLANES, TM), lambda i: (0, i)),
                   pl.BlockSpec((N_EXPERTS, LANES), lambda i: (0, 0))],
        out_shape=[jax.ShapeDtypeStruct((SUBLANES, n), I32), jax.ShapeDtypeStruct((SUBLANES, n), F32),
                   jax.ShapeDtypeStruct((N_EXPERTS, LANES), F32)],
        scratch_shapes=[pltpu.VMEM((N_EXPERTS, TM), F32), pltpu.VMEM((N_EXPERTS, 1), F32)],
        compiler_params=_cparams(("arbitrary",)),
        name="router",
    )(x, wr_t, rb)


def _row_copy(src, src_row, dst, dst_row, sem):
    return pltpu.make_async_copy(src.at[pl.ds(src_row, 1)], dst.at[pl.ds(dst_row, 1)], sem)


def _dispatch_kernel(dlo_ref, dhi_ref, pad_ref, x_ref, xs_hbm, zero_ref, sem):
    i = pl.program_id(0)

    @pl.when(i == 0)
    def _():
        zero_ref[...] = jnp.zeros(zero_ref.shape, F32)
        for j in range(2 * N_EXPERTS):
            @pl.when(pad_ref[j] >= 0)
            def _():
                first = pl.multiple_of(pad_ref[j], TM_MOE)
                cp = pltpu.make_async_copy(zero_ref, xs_hbm.at[pl.ds(first, TM_MOE)], sem.at[2])
                cp.start()
                cp.wait()

    base = i * TM

    def body(g, c):
        for u in range(ROW_UNROLL):
            r = g * ROW_UNROLL + u
            t = base + r
            _row_copy(x_ref, r, xs_hbm, dlo_ref[t], sem.at[0]).start(priority=u % 2)
            _row_copy(x_ref, r, xs_hbm, dhi_ref[t], sem.at[1]).start(priority=(u + 1) % 2)
        return c

    lax.fori_loop(0, TM // ROW_UNROLL, body, 0)
    pltpu.make_async_copy(x_ref, xs_hbm.at[pl.ds(0, TM)], sem.at[0]).wait()
    pltpu.make_async_copy(x_ref, xs_hbm.at[pl.ds(0, TM)], sem.at[1]).wait()


def _dispatch_call(dlo, dhi, pad_start, xf, n_rows):
    n, d = xf.shape
    return pl.pallas_call(
        _dispatch_kernel,
        grid_spec=pltpu.PrefetchScalarGridSpec(
            num_scalar_prefetch=3,
            grid=(n // TM,),
            in_specs=[pl.BlockSpec((TM, d), lambda i, a, b, c: (i, 0))],
            out_specs=pl.BlockSpec(memory_space=pl.ANY),
            scratch_shapes=[pltpu.VMEM((TM_MOE, d), F32), pltpu.SemaphoreType.DMA((3,))]),
        out_shape=jax.ShapeDtypeStruct((n_rows, d), F32),
        compiler_params=_cparams(("arbitrary",)),
        name="moe_dispatch",
    )(dlo, dhi, pad_start, xf)


def _experts_kernel(te_ref, nt_ref, xs_ref, wg_ref, wu_ref, wd_ref, ys_ref, wgb_ref, wub_ref, wdb_ref):
    i = pl.program_id(0)
    used = i < nt_ref[0]

    @pl.when(jnp.logical_or(i == 0, te_ref[i] != te_ref[jnp.maximum(i - 1, 0)]))
    def _():
        wgb_ref[...] = wg_ref[0].astype(BF16)
        wub_ref[...] = wu_ref[0].astype(BF16)
        wdb_ref[...] = wd_ref[0].astype(BF16)

    @pl.when(used)
    def _():
        x = xs_ref[...].astype(BF16)
        a = _dot(x, wgb_ref[...])
        u = _dot(x, wub_ref[...])
        h = a * jax.nn.sigmoid(a) * u
        ys_ref[...] = _dot(h.astype(BF16), wdb_ref[...])

    @pl.when(jnp.logical_not(used))
    def _():
        ys_ref[...] = jnp.zeros(ys_ref.shape, F32)


def _experts_call(tile_expert, n_tiles, xs, wg, wu, wd):
    n_rows, d = xs.shape
    tile = lambda i, te, nt: (jnp.maximum(jnp.minimum(i, nt[0] - 1), 0), 0)
    return pl.pallas_call(
        _experts_kernel,
        grid_spec=pltpu.PrefetchScalarGridSpec(
            num_scalar_prefetch=2,
            grid=(n_rows // TM_MOE,),
            in_specs=[pl.BlockSpec((TM_MOE, d), tile),
                      pl.BlockSpec((1, d, D_EXPERT), lambda i, te, nt: (te[i], 0, 0)),
                      pl.BlockSpec((1, d, D_EXPERT), lambda i, te, nt: (te[i], 0, 0)),
                      pl.BlockSpec((1, D_EXPERT, d), lambda i, te, nt: (te[i], 0, 0))],
            out_specs=pl.BlockSpec((TM_MOE, d), lambda i, te, nt: (i, 0)),
            scratch_shapes=[pltpu.VMEM((d, D_EXPERT), BF16), pltpu.VMEM((d, D_EXPERT), BF16),
                            pltpu.VMEM((D_EXPERT, d), BF16)]),
        out_shape=jax.ShapeDtypeStruct((n_rows, d), F32),
        compiler_params=_cparams(("arbitrary",)),
        name="moe_experts",
    )(tile_expert, n_tiles, xs, wg, wu, wd)


def _combine_kernel(dlo_ref, dhi_ref, ys_hbm, xf_ref, w_ref, lg_ref, lb_ref, of_ref, ob_ref, a_ref, b_ref, sem):
    i = pl.program_id(0)
    n = pl.num_programs(0)

    def issue(tile, slot):
        base = tile * TM_CMB

        def body(g, c):
            for u in range(ROW_UNROLL):
                r = g * ROW_UNROLL + u
                t = base + r
                _row_copy(ys_hbm, dlo_ref[t], a_ref.at[slot], r, sem.at[0, slot]).start(priority=u % 2)
                _row_copy(ys_hbm, dhi_ref[t], b_ref.at[slot], r, sem.at[1, slot]).start(priority=(u + 1) % 2)
            return c

        lax.fori_loop(0, TM_CMB // ROW_UNROLL, body, 0)

    @pl.when(i == 0)
    def _():
        issue(0, 0)

    @pl.when(i + 1 < n)
    def _():
        issue(i + 1, (i + 1) % 2)

    slot = i % 2
    pltpu.make_async_copy(ys_hbm.at[pl.ds(0, TM_CMB)], a_ref.at[slot], sem.at[0, slot]).wait()
    pltpu.make_async_copy(ys_hbm.at[pl.ds(0, TM_CMB)], b_ref.at[slot], sem.at[1, slot]).wait()
    w = w_ref[...]
    ffn = w[:, 0:1] * a_ref[slot] + w[:, 1:2] * b_ref[slot]
    y = _layer_norm(ALPHA * xf_ref[...] + ffn, lg_ref[...], lb_ref[...])
    of_ref[...] = y
    ob_ref[...] = y.astype(BF16)


def _combine_call(dlo, dhi, ys, xf, w2, lg, lb):
    n, d = xf.shape
    row = lambda wdt: pl.BlockSpec((TM_CMB, wdt), lambda i, a, b: (i, 0))
    vec = pl.BlockSpec((1, d), lambda i, a, b: (0, 0))
    return pl.pallas_call(
        _combine_kernel,
        grid_spec=pltpu.PrefetchScalarGridSpec(
            num_scalar_prefetch=2,
            grid=(n // TM_CMB,),
            in_specs=[pl.BlockSpec(memory_space=pl.ANY), row(d), row(2), vec, vec],
            out_specs=[row(d), row(d)],
            scratch_shapes=[pltpu.VMEM((2, TM_CMB, d), F32), pltpu.VMEM((2, TM_CMB, d), F32),
                            pltpu.SemaphoreType.DMA((2, 2))]),
        out_shape=[jax.ShapeDtypeStruct((n, d), F32), jax.ShapeDtypeStruct((n, d), BF16)],
        compiler_params=_cparams(("arbitrary",)),
        name="moe_combine",
    )(dlo, dhi, ys, xf, w2, lg, lb)


def _moe(xf, wr_t, rb, wg, wu, wd, expert_base, lg, lb):
    n, d = xf.shape
    idx, wgt, cnt = _router_call(xf, wr_t, rb)
    counts = cnt[:, 0].astype(I32)
    padded = (counts + TM_MOE - 1) // TM_MOE * TM_MOE
    ends = jnp.cumsum(padded)
    starts = ends - padded
    max_tiles = (TOP_K * n) // TM_MOE + N_EXPERTS
    n_tiles = (ends[-1] // TM_MOE).astype(I32).reshape(1)
    tile_ids = jnp.arange(max_tiles, dtype=I32)
    tile_expert = jnp.minimum(jnp.sum((tile_ids[:, None] >= (ends // TM_MOE)[None, :]).astype(I32), axis=1),
                              N_EXPERTS - 1)
    onehot = lambda e: (e[None, :] == jnp.arange(N_EXPERTS, dtype=I32)[:, None])
    dlo = jnp.sum(jnp.where(onehot(idx[0]), starts[:, None], 0), axis=0) + idx[2]
    dhi = jnp.sum(jnp.where(onehot(idx[1]), starts[:, None], 0), axis=0) + idx[3]
    tail = n_tiles[0] + jnp.arange(N_EXPERTS, dtype=I32)
    pad_start = jnp.concatenate([jnp.where(counts % TM_MOE != 0, ends - TM_MOE, -1),
                                 jnp.where(tail < max_tiles, tail * TM_MOE, -1)]).astype(I32)
    xs = _dispatch_call(dlo, dhi, pad_start, xf, max_tiles * TM_MOE)
    ys = _experts_call(tile_expert + expert_base, n_tiles, xs, wg, wu, wd)
    return _combine_call(dlo, dhi, ys, xf, wgt[:2].T, lg, lb)


def _rope_tables(seq):
    pos = jnp.arange(seq, dtype=F32)[:, None]
    lane = np.arange(LANES)

    def angles(half):
        inv = ROPE_THETA ** (-jnp.arange(half, dtype=F32) / half)
        return pos * inv[None, :]

    def lane_table(half, lo, hi):
        ang = angles(half)
        idx = (lane - lo) % half
        active = (lane >= lo) & (lane < hi)
        first = ((lane - lo) % (2 * half)) < half
        cos = jnp.where(active[None, :], jnp.cos(ang)[:, idx], 1.0)
        sin = jnp.where(active[None, :], jnp.sin(ang)[:, idx] * np.where(first, -1.0, 1.0)[None, :], 0.0)
        return cos.astype(F32), sin.astype(F32)

    def feat_table(half):
        ang = angles(half)
        return jnp.cos(ang).T.astype(F32), jnp.sin(ang).T.astype(F32)

    c64, s64 = lane_table(B_HDIM // 2, 0, LANES)
    c64t, s64t = feat_table(B_HDIM // 2)
    ck, sk = lane_table(A_ROPE // 2, 0, A_ROPE)
    cqt, sqt = feat_table(A_ROPE // 2)
    return (c64, s64, c64t, s64t), (cqt, sqt, ck, sk)


def _layer_weights(l, w_in, a_w_uq, a_w_ukv):
    d = D_MODEL
    pts = [int(p) for p in np.cumsum(SPLIT_SIZES)[:-1]]
    (a_q, a_kv, a_kpe, b_q, b_k, b_v, b_qi, b_ki, b_wi, c_q, c_k, c_v, gates) = jnp.split(w_in[l], pts, axis=1)
    zeros = lambda n: jnp.zeros((d, n), F32)
    bf = lambda a: a.astype(BF16)
    w_mla = jnp.concatenate([a_q, a_kv, a_kpe, zeros(LANES - A_ROPE)], axis=1)
    dsa_w = jnp.concatenate([b_k, b_ki, zeros(LANES - IDX_DIM)], axis=1)
    qi_t = jnp.concatenate([b_qi.reshape(d, IDX_HEADS, IDX_DIM),
                            jnp.zeros((d, IDX_HEADS, LANES - IDX_DIM), F32)], axis=2).reshape(d, -1).T
    wi_t = jnp.concatenate([b_wi * (IDX_HEADS * IDX_DIM) ** -0.5, zeros(WI_ROWS - IDX_HEADS)], axis=1).T
    dsa_wt = jnp.concatenate([(b_q * (B_HDIM ** -0.5 * LOG2E)).T, b_v.T, qi_t, wi_t], axis=0)
    swa_wt = jnp.concatenate([(c_q * (C_HDIM ** -0.5 * LOG2E)).T, c_v.T], axis=0)
    uq = a_w_uq[l].reshape(A_Q_RANK, A_HEADS, A_NOPE + A_ROPE) * ((A_NOPE + A_ROPE) ** -0.5 * LOG2E)
    wq = jnp.concatenate([uq, jnp.zeros((A_Q_RANK, A_HEADS, LANES - A_NOPE - A_ROPE), F32)], axis=2)
    ukv = a_w_ukv[l].reshape(A_KV_RANK, A_HEADS, A_NOPE + A_VDIM)
    wk = jnp.concatenate([ukv[:, :, :A_NOPE], jnp.zeros((A_KV_RANK, A_HEADS, LANES - A_NOPE), F32)], axis=2)
    wv = ukv[:, :, A_NOPE:]
    return dict(w_mla=bf(w_mla), dsa_w=bf(dsa_w), dsa_wt=bf(dsa_wt), swa_w=bf(c_k), swa_wt=bf(swa_wt),
                w_gate=bf(gates), wqt=bf(wq.reshape(A_Q_RANK, -1).T), wk=bf(wk.reshape(A_KV_RANK, -1)),
                wvt=bf(wv.reshape(A_KV_RANK, -1).T))


QW = B_HEADS * B_HDIM
DSA_TOK = ((0, LANES, True), (LANES, LANES, True))
DSA_FEAT = ((0, QW, True, "wide"), (QW, LANES, False, TK), (QW + LANES, IDX_HEADS * LANES, True, 0),
            (QW + LANES + IDX_HEADS * LANES, WI_ROWS, False, 0))
SWA_TOK = ((0, LANES, True),)
SWA_FEAT = ((0, QW, True, "wide"), (QW, LANES, False, SWA_KB))


def kernel(x, ln_in_g, ln_in_b, w_in, a_q_ln_g, a_kv_ln_g, a_w_uq, a_w_ukv, c_sinks, w_br_a, w_br_b, w_br_c,
           w_out, ln1_g, ln1_b, w_router, router_bias, w_exp_gate, w_exp_up, w_exp_down, ln2_g, ln2_b):
    batch, seq, d = x.shape
    assert d == D_MODEL and seq % TM == 0 and seq % TQ == 0 and TQ == TK
    n = batch * seq
    tabs64, mla_tabs = _rope_tables(seq)
    xf, xb = _ln_call(x.reshape(n, d), ln_in_g, ln_in_b)
    wr_t = w_router.T
    rb = router_bias.reshape(N_EXPERTS, 1)
    wg_all = w_exp_gate.reshape(DEPTH * N_EXPERTS, d, D_EXPERT)
    wu_all = w_exp_up.reshape(DEPTH * N_EXPERTS, d, D_EXPERT)
    wd_all = w_exp_down.reshape(DEPTH * N_EXPERTS, D_EXPERT, d)
    bf = lambda a: a.astype(BF16)
    vec = lambda a: a.reshape(1, -1)
    for l in range(DEPTH):
        w = _layer_weights(l, w_in, a_w_uq, a_w_ukv)
        qa, ka, va = _mla_proj_call(xb, w["w_mla"], vec(a_q_ln_g[l]), vec(a_kv_ln_g[l]), w["wqt"], w["wk"],
                                    w["wvt"], mla_tabs, seq)
        o_a = _mla_attn_call(qa, ka, va, batch, seq)
        bk, bki, bq, bv, bqi, bwi = _qkv_proj_call(xb, w["dsa_w"], w["dsa_wt"], tabs64, DSA_TOK, (BF16, BF16),
                                                   DSA_FEAT, (BF16, BF16, BF16, F32), "dsa_proj", seq)
        o_b = _dsa_attn_call(bq, bk, bv, bqi, bki, bwi, batch, seq)
        ck_, cq_, cv_ = _qkv_proj_call(xb, w["swa_w"], w["swa_wt"], tabs64, SWA_TOK, (BF16,),
                                       SWA_FEAT, (BF16, BF16), "swa_proj", seq)
        o_c = _swa_attn_call(c_sinks[l], cq_, ck_, cv_, batch, seq)
        g0, g1, g2 = _gate_proj_call(xb, w["w_gate"])
        xf, xb = _merge_call(o_a, o_b, o_c, g0, g1, g2, xf, bf(w_br_a[l]), bf(w_br_b[l]), bf(w_br_c[l]),
                             bf(w_out[l]), vec(ln1_g[l]), vec(ln1_b[l]))
        xf, xb = _moe(xf, wr_t, rb, wg_all, wu_all, wd_all, l * N_EXPERTS, vec(ln2_g[l]), vec(ln2_b[l]))
    return xf.reshape(batch, seq, d)
```

```python
import functools

import jax
import jax.numpy as jnp
import numpy as np
from jax import lax
from jax.experimental import pallas as pl
from jax.experimental.pallas import tpu as pltpu

F32 = jnp.float32
BF16 = jnp.bfloat16
I32 = jnp.int32

D_MODEL = 1024
DEPTH = 2
CHUNK = 64
ROPE_THETA = 10000.0
LN_EPS = 1e-5
RMS_EPS = 1e-6
NEG = -1e30

A_HEADS, A_NOPE, A_ROPE, A_VDIM, A_Q_RANK, A_KV_RANK = 8, 64, 32, 64, 384, 256
B_HEADS, B_KV_HEADS, B_HDIM, IDX_HEADS, IDX_DIM, TOPK_MAX = 8, 2, 64, 4, 64, 256
C_HEADS, C_KV_HEADS, C_HDIM, WINDOW = 8, 2, 64, 128
WIN_CHUNKS = WINDOW // CHUNK
N_EXPERTS, N_GROUPS, TOP_K, D_EXPERT = 16, 4, 2, 512
EXPERTS_PER_GROUP = N_EXPERTS // N_GROUPS
ALPHA = (2.0 * DEPTH) ** 0.25
LOG2E = 1.4426950408889634

SPLIT_SIZES = (
    A_Q_RANK, A_KV_RANK, A_ROPE,
    B_HEADS * B_HDIM, B_KV_HEADS * B_HDIM, B_KV_HEADS * B_HDIM,
    IDX_HEADS * IDX_DIM, IDX_DIM, IDX_HEADS,
    C_HEADS * C_HDIM, C_KV_HEADS * C_HDIM, C_KV_HEADS * C_HDIM,
    3 * D_MODEL,
)

LANES = 128
SUBLANES = 8
TM = 512
TQ = 256
TK = 256
SWA_KB = WIN_CHUNKS * CHUNK
TM_MOE = 512
TM_CMB = 256
ROW_UNROLL = 8
WI_ROWS = 16
VMEM_LIMIT = 56 * 1024 * 1024
F32_TINY = 1.1754943508222875e-38


def _cparams(sem):
    return pltpu.CompilerParams(dimension_semantics=sem, vmem_limit_bytes=VMEM_LIMIT)


def _dot(a, b):
    return jnp.dot(a, b, preferred_element_type=F32)


def _dot_nt(a, b):
    return lax.dot_general(a, b, (((1,), (1,)), ((), ())), preferred_element_type=F32)


def _layer_norm(z, g, b):
    mu = jnp.mean(z, axis=-1, keepdims=True)
    zc = z - mu
    var = jnp.mean(zc * zc, axis=-1, keepdims=True)
    return zc * lax.rsqrt(var + LN_EPS) * g + b


def _ln_kernel(x_ref, g_ref, b_ref, of_ref, ob_ref):
    y = _layer_norm(x_ref[...], g_ref[...], b_ref[...])
    of_ref[...] = y
    ob_ref[...] = y.astype(BF16)


def _ln_call(x, g, b):
    n, d = x.shape
    row = pl.BlockSpec((TM, d), lambda i: (i, 0))
    vec = pl.BlockSpec((1, d), lambda i: (0, 0))
    return pl.pallas_call(
        _ln_kernel,
        grid=(n // TM,),
        in_specs=[row, vec, vec],
        out_specs=[row, row],
        out_shape=[jax.ShapeDtypeStruct((n, d), F32), jax.ShapeDtypeStruct((n, d), BF16)],
        compiler_params=_cparams(("parallel",)),
        name="ln_in",
    )(x, g.reshape(1, d), b.reshape(1, d))


def _rope_lanes(y, cos, sin, lane, half):
    sw = jnp.where((lane & (2 * half - 1)) < half, pltpu.roll(y, LANES - half, 1), pltpu.roll(y, half, 1))
    return y * cos + sw * sin


def _qkv_proj_kernel(tok_segs, feat_segs, x_ref, w_ref, wt_ref, cos_ref, sin_ref, cost_ref, sint_ref, *out_refs):
    x = x_ref[...]
    n_tok = len(tok_segs)
    cos = cos_ref[...]
    sin = sin_ref[...]
    lane = lax.broadcasted_iota(I32, cos.shape, 1)
    for (start, width, rope), o_ref in zip(tok_segs, out_refs[:n_tok]):
        y = _dot(x, w_ref[:, start:start + width])
        for c in range(width // LANES):
            yc = y[:, c * LANES:(c + 1) * LANES]
            if rope:
                yc = _rope_lanes(yc, cos, sin, lane, B_HDIM // 2)
            o_ref[:, c * LANES:(c + 1) * LANES] = yc.astype(o_ref.dtype)
    ct = cost_ref[...]
    st = sint_ref[...]
    half = B_HDIM // 2
    for (start, rows, rope, kb), o_ref in zip(feat_segs, out_refs[n_tok:]):
        yt = _dot_nt(wt_ref[start:start + rows, :], x)
        if kb == "wide":
            n_heads = rows // B_HDIM
            rep = n_heads // (LANES // B_HDIM)
            zero = jnp.zeros((B_HDIM, TQ), o_ref.dtype)
            for h in range(n_heads):
                x0 = yt[h * B_HDIM:h * B_HDIM + half]
                x1 = yt[h * B_HDIM + half:(h + 1) * B_HDIM]
                r0 = (x0 * ct - x1 * st).astype(o_ref.dtype)
                r1 = (x1 * ct + x0 * st).astype(o_ref.dtype)
                g = h // rep
                for c in range(yt.shape[1] // TQ):
                    cs = slice(c * TQ, (c + 1) * TQ)
                    hs = slice(h * TQ, (h + 1) * TQ)
                    o_ref[c, g * B_HDIM:g * B_HDIM + half, hs] = r0[:, cs]
                    o_ref[c, g * B_HDIM + half:(g + 1) * B_HDIM, hs] = r1[:, cs]
                    o_ref[c, (1 - g) * B_HDIM:(2 - g) * B_HDIM, hs] = zero
        elif rope:
            for r in range(rows // B_HDIM):
                x0 = yt[r * B_HDIM:r * B_HDIM + half]
                x1 = yt[r * B_HDIM + half:(r + 1) * B_HDIM]
                o_ref[r * B_HDIM:r * B_HDIM + half, :] = (x0 * ct - x1 * st).astype(o_ref.dtype)
                o_ref[r * B_HDIM + half:(r + 1) * B_HDIM, :] = (x1 * ct + x0 * st).astype(o_ref.dtype)
        elif kb:
            for c in range(yt.shape[1] // kb):
                o_ref[c] = yt[:, c * kb:(c + 1) * kb].astype(o_ref.dtype)
        else:
            o_ref[...] = yt.astype(o_ref.dtype)


def _qkv_proj_call(xb, w, wt, tabs, tok_segs, tok_dtypes, feat_segs, feat_dtypes, name, seq):
    n, d = xb.shape
    tpb = seq // TM
    cos, sin, cost, sint = tabs
    full = lambda a: pl.BlockSpec(a.shape, lambda i: (0, 0))
    out_specs, out_shape = [], []
    for (start, width, rope), dt in zip(tok_segs, tok_dtypes):
        out_specs.append(pl.BlockSpec((TM, width), lambda i: (i, 0)))
        out_shape.append(jax.ShapeDtypeStruct((n, width), dt))
    for (start, rows, rope, kb), dt in zip(feat_segs, feat_dtypes):
        if kb == "wide":
            wide = (rows // B_HDIM) * TQ
            out_specs.append(pl.BlockSpec((TM // TQ, LANES, wide), lambda i: (i, 0, 0)))
            out_shape.append(jax.ShapeDtypeStruct((n // TQ, LANES, wide), dt))
        elif kb:
            out_specs.append(pl.BlockSpec((TM // kb, rows, kb), lambda i: (i, 0, 0)))
            out_shape.append(jax.ShapeDtypeStruct((n // kb, rows, kb), dt))
        else:
            out_specs.append(pl.BlockSpec((rows, TM), lambda i: (0, i)))
            out_shape.append(jax.ShapeDtypeStruct((rows, n), dt))
    return pl.pallas_call(
        functools.partial(_qkv_proj_kernel, tok_segs, feat_segs),
        grid=(n // TM,),
        in_specs=[pl.BlockSpec((TM, d), lambda i: (i, 0)), full(w), full(wt),
                  pl.BlockSpec((TM, LANES), lambda i: (i % tpb, 0)),
                  pl.BlockSpec((TM, LANES), lambda i: (i % tpb, 0)),
                  pl.BlockSpec((B_HDIM // 2, TM), lambda i: (0, i % tpb)),
                  pl.BlockSpec((B_HDIM // 2, TM), lambda i: (0, i % tpb))],
        out_specs=out_specs,
        out_shape=out_shape,
        compiler_params=_cparams(("parallel",)),
        name=name,
    )(xb, w, wt, cos, sin, cost, sint)


def _gate_proj_kernel(x_ref, w_ref, *out_refs):
    x = x_ref[...]
    for c, o_ref in enumerate(out_refs):
        y = _dot(x, w_ref[:, c * D_MODEL:(c + 1) * D_MODEL])
        o_ref[...] = jax.nn.sigmoid(y).astype(o_ref.dtype)


def _gate_proj_call(xb, w):
    n, d = xb.shape
    row = pl.BlockSpec((TM, d), lambda i: (i, 0))
    return pl.pallas_call(
        _gate_proj_kernel,
        grid=(n // TM,),
        in_specs=[row, pl.BlockSpec(w.shape, lambda i: (0, 0))],
        out_specs=[row] * 3,
        out_shape=[jax.ShapeDtypeStruct((n, d), BF16)] * 3,
        compiler_params=_cparams(("parallel",)),
        name="gate_proj",
    )(xb, w)


def _mla_proj_kernel(x_ref, w1_ref, qg_ref, kvg_ref, wqt_ref, wk_ref, wvt_ref,
                     cqt_ref, sqt_ref, ck_ref, sk_ref, qt_ref, k_ref, vt_ref):
    y = _dot(x_ref[...], w1_ref[...])
    ql = y[:, :A_Q_RANK]
    kvl = y[:, A_Q_RANK:A_Q_RANK + A_KV_RANK]
    kpe = y[:, A_Q_RANK + A_KV_RANK:]
    qn = (ql * lax.rsqrt(jnp.mean(ql * ql, axis=-1, keepdims=True) + RMS_EPS) * qg_ref[...]).astype(BF16)
    kvn = (kvl * lax.rsqrt(jnp.mean(kvl * kvl, axis=-1, keepdims=True) + RMS_EPS) * kvg_ref[...]).astype(BF16)
    half = A_ROPE // 2
    qt = _dot_nt(wqt_ref[...], qn)
    ct = cqt_ref[...]
    st = sqt_ref[...]
    for h in range(A_HEADS):
        base = h * LANES
        r0, r1, r2 = base + A_NOPE, base + A_NOPE + half, base + A_NOPE + A_ROPE
        x0 = qt[r0:r1]
        x1 = qt[r1:r2]
        qt_ref[base:r0, :] = qt[base:r0].astype(BF16)
        qt_ref[r0:r1, :] = (x0 * ct - x1 * st).astype(BF16)
        qt_ref[r1:r2, :] = (x1 * ct + x0 * st).astype(BF16)
        qt_ref[r2:base + LANES, :] = qt[r2:base + LANES].astype(BF16)
    kn = _dot(kvn, wk_ref[...])
    lane = lax.broadcasted_iota(I32, kpe.shape, 1)
    kpe_sw = jnp.where(lane < half, pltpu.roll(kpe, LANES - half, 1), pltpu.roll(kpe, half, 1))
    kpe_p = pltpu.roll(kpe * ck_ref[...] + kpe_sw * sk_ref[...], A_NOPE, 1)
    for h in range(A_HEADS):
        sl = slice(h * LANES, (h + 1) * LANES)
        k_ref[:, sl] = (kn[:, sl] + kpe_p).astype(BF16)
    vt = _dot_nt(wvt_ref[...], kvn)
    for c in range(vt.shape[1] // TK):
        vt_ref[c] = vt[:, c * TK:(c + 1) * TK].astype(BF16)


def _mla_proj_call(xb, w1, qg, kvg, wqt, wk, wvt, tabs, seq):
    n, d = xb.shape
    tpb = seq // TM
    cqt, sqt, ck, sk = tabs
    full = lambda a: pl.BlockSpec(a.shape, lambda i: (0, 0))
    ttab = pl.BlockSpec((A_ROPE // 2, TM), lambda i: (0, i % tpb))
    ltab = pl.BlockSpec((TM, LANES), lambda i: (i % tpb, 0))
    hw = A_HEADS * LANES
    vw = A_HEADS * A_VDIM
    return pl.pallas_call(
        _mla_proj_kernel,
        grid=(n // TM,),
        in_specs=[pl.BlockSpec((TM, d), lambda i: (i, 0)), full(w1), full(qg), full(kvg), full(wqt), full(wk),
                  full(wvt), ttab, ttab, ltab, ltab],
        out_specs=[pl.BlockSpec((hw, TM), lambda i: (0, i)),
                   pl.BlockSpec((TM, hw), lambda i: (i, 0)),
                   pl.BlockSpec((TM // TK, vw, TK), lambda i: (i, 0, 0))],
        out_shape=[jax.ShapeDtypeStruct((hw, n), BF16), jax.ShapeDtypeStruct((n, hw), BF16),
                   jax.ShapeDtypeStruct((n // TK, vw, TK), BF16)],
        compiler_params=_cparams(("parallel",)),
        name="mla_proj",
    )(xb, w1, qg, kvg, wqt, wk, wvt, cqt, sqt, ck, sk)


def _chunk_of(pos):
    return lax.shift_right_logical(pos, 6)


def _diag_visible():
    k = lax.broadcasted_iota(I32, (TK, TQ), 0)
    q = lax.broadcasted_iota(I32, (TK, TQ), 1)
    return _chunk_of(k) <= _chunk_of(q)


def _online_step(h, s, vt, m_ref, l_ref, acc_ref):
    for c in range(s.shape[1] // LANES):
        cs = slice(c * LANES, (c + 1) * LANES)
        sc = s[:, cs]
        m_prev = m_ref[h, :, cs]
        m_new = jnp.maximum(m_prev, jnp.max(sc, axis=0, keepdims=True))
        alpha = jnp.exp2(m_prev - m_new)
        p = jnp.exp2(sc - m_new)
        l_ref[h, :, cs] = alpha * l_ref[h, :, cs] + jnp.sum(p, axis=0, keepdims=True)
        acc_ref[h, :, cs] = alpha * acc_ref[h, :, cs] + _dot(vt, p.astype(BF16))
        m_ref[h, :, cs] = m_new


def _init_state(m_ref, l_ref, acc_ref):
    m_ref[...] = jnp.full(m_ref.shape, NEG, F32)
    l_ref[...] = jnp.zeros(l_ref.shape, F32)
    acc_ref[...] = jnp.zeros(acc_ref.shape, F32)


def _write_heads(o_ref, l_ref, acc_ref, n_heads):
    for j in range(n_heads // 2):
        ot = jnp.concatenate([acc_ref[2 * j] / l_ref[2 * j], acc_ref[2 * j + 1] / l_ref[2 * j + 1]], axis=0)
        o_ref[:, j * LANES:(j + 1) * LANES] = ot.T.astype(o_ref.dtype)


def _mla_attn_kernel(qt_ref, k_ref, vt_ref, o_ref, m_ref, l_ref, acc_ref):
    i = pl.program_id(1)
    _init_state(m_ref, l_ref, acc_ref)
    diag = _diag_visible()

    def block(j, masked):
        rows = pl.ds(pl.multiple_of(j * TK, TK), TK)
        vt = vt_ref[j]
        ss = [_dot(k_ref[rows, h * LANES:(h + 1) * LANES], qt_ref[h * LANES:(h + 1) * LANES, :])
              for h in range(A_HEADS)]
        for h in range(A_HEADS):
            s = jnp.where(diag, ss[h], NEG) if masked else ss[h]
            _online_step(h, s, vt[h * A_VDIM:(h + 1) * A_VDIM], m_ref, l_ref, acc_ref)

    def body(j, c):
        block(j, False)
        return c

    lax.fori_loop(0, i, body, 0)
    block(i, True)
    _write_heads(o_ref, l_ref, acc_ref, A_HEADS)


def _attn_scratch(n_heads, dv):
    return [pltpu.VMEM((n_heads, 1, TQ), F32), pltpu.VMEM((n_heads, 1, TQ), F32),
            pltpu.VMEM((n_heads, dv, TQ), F32)]


def _mla_attn_call(qt, k, vt, batch, seq):
    nq = seq // TQ
    nkb = seq // TK
    hw = A_HEADS * LANES
    vw = A_HEADS * A_VDIM
    return pl.pallas_call(
        _mla_attn_kernel,
        grid=(batch, nq),
        in_specs=[pl.BlockSpec((hw, TQ), lambda b, i: (0, b * nq + i)),
                  pl.BlockSpec((seq, hw), lambda b, i: (b, 0)),
                  pl.BlockSpec((nkb, vw, TK), lambda b, i: (b, 0, 0))],
        out_specs=pl.BlockSpec((TQ, vw), lambda b, i: (b * nq + i, 0)),
        out_shape=jax.ShapeDtypeStruct((batch * seq, vw), BF16),
        scratch_shapes=_attn_scratch(A_HEADS, A_VDIM),
        compiler_params=_cparams(("parallel", "arbitrary")),
        name="mla_attn",
    )(qt, k, vt)


def _dsa_attn_kernel(n_sel, qw_ref, k_ref, vt_ref, qit_ref, ki_ref, wt_ref, o_ref,
                     key_ref, part_ref, bias_ref, j_ref, m_ref, l_ref, acc_ref):
    i = pl.program_id(1)
    nblk = i + 1
    diag = _diag_visible()
    kpos = lax.broadcasted_iota(I32, (TK, TQ), 0)
    w = wt_ref[...]

    def score_block(jb, c):
        ki = ki_ref[pl.ds(pl.multiple_of(jb * TK, TK), TK), :]
        sc = jnp.zeros((TK, TQ), F32)
        for h in range(IDX_HEADS):
            rel = jnp.maximum(_dot(ki, qit_ref[h * LANES:(h + 1) * LANES, :]), 0.0)
            sc = sc + w[h:h + 1, :] * rel
        sc = jnp.where(jnp.logical_or(jb < i, diag), sc, NEG)
        sc = jnp.where(jnp.abs(sc) < F32_TINY, 0.0, sc)
        bits = lax.bitcast_convert_type(sc, I32)
        key_ref[jb] = jnp.where(bits < 0, bits ^ 0x7FFFFFFF, bits)
        part_ref[jb] = lax.bitcast_convert_type(bits & -65536, F32).astype(BF16)
        return c

    lax.fori_loop(0, nblk, score_block, 0)

    one16 = jnp.ones((TK, TQ), BF16)
    zero16 = jnp.zeros((TK, TQ), BF16)
    rows16 = 2 * SUBLANES

    def count16(pred):
        def body(jb, acc):
            ind = jnp.where(pred(part_ref[jb]), one16, zero16).reshape(TK // rows16, rows16, TQ)
            parts = [ind[r] for r in range(TK // rows16)]
            while len(parts) > 1:
                parts = [parts[a] + parts[a + 1] for a in range(0, len(parts), 2)]
            return acc + parts[0]
        acc = lax.fori_loop(0, nblk, body, jnp.zeros((rows16, TQ), BF16))
        return jnp.sum(acc.astype(F32), axis=0, keepdims=True).astype(I32)

    def greedy(n_bits, start, to_float, want):
        def step(s, v):
            cand = v | lax.shift_left(jnp.int32(1), n_bits - 1 - s)
            cf = to_float(cand)
            c = count16(lambda x: x >= cf)
            return jnp.where(c >= want, cand, v)
        return lax.fori_loop(0, n_bits, step, start)

    def upper_float(kh):
        p = jnp.where(kh >= 0, kh, kh ^ 0x7FFF)
        p = jnp.where(jnp.logical_and(p > 0, p < 0x80), 0x80, p)
        return lax.bitcast_convert_type(lax.shift_left(p, 16), F32).astype(BF16)

    small_float = lambda v: v.astype(F32).astype(BF16)

    c0 = count16(lambda x: x >= jnp.zeros((1, TQ), BF16))
    th = greedy(15, jnp.where(c0 >= n_sel, 0, -32768).astype(I32), upper_float, n_sel)
    thf = upper_float(th)
    need_m = n_sel - count16(lambda x: x > thf)

    def mid_block(jb, c):
        k = key_ref[jb]
        band = lax.shift_right_arithmetic(k, 16) == th
        mid = (lax.shift_right_logical(k, 8) & 0xFF).astype(F32)
        part_ref[jb] = jnp.where(band, mid, -1.0).astype(BF16)
        return c

    lax.fori_loop(0, nblk, mid_block, 0)
    tm = greedy(8, jnp.zeros((1, TQ), I32), small_float, need_m)
    tmf = small_float(tm)
    need_l = need_m - count16(lambda x: x > tmf)

    t24 = lax.shift_left(th, 8) | tm

    def low_block(jb, c):
        k = key_ref[jb]
        band = lax.shift_right_arithmetic(k, 8) == t24
        part_ref[jb] = jnp.where(band, (k & 0xFF).astype(F32), -1.0).astype(BF16)
        return c

    lax.fori_loop(0, nblk, low_block, 0)
    tl = greedy(8, jnp.zeros((1, TQ), I32), small_float, need_l)
    tlf = small_float(tl)
    t = lax.shift_left(t24, 8) | tl
    need = need_l - count16(lambda x: x > tlf)
    c_eq = count16(lambda x: x == tlf)

    j_ref[...] = jnp.full(j_ref.shape, 2 ** 30, I32)

    @pl.when(jnp.max(c_eq - need) > 0)
    def _():
        def scan(jb, carry):
            cum, blk, want = carry
            ind = jnp.where(part_ref[jb] == tlf, one16, zero16).reshape(TK // rows16, rows16, TQ)
            parts = [ind[r] for r in range(TK // rows16)]
            while len(parts) > 1:
                parts = [parts[a] + parts[a + 1] for a in range(0, len(parts), 2)]
            new = cum + jnp.sum(parts[0].astype(F32), axis=0, keepdims=True).astype(I32)
            hit = jnp.logical_and(cum < need, new >= need)
            return new, jnp.where(hit, jb, blk), jnp.where(hit, need - cum, want)

        zero_q = jnp.zeros((1, TQ), I32)
        _, blk, want = lax.fori_loop(0, nblk, scan, (zero_q, zero_q, zero_q + 1))

        row16 = kpos.astype(F32).astype(BF16)
        far16 = jnp.full((TK, TQ), 2.0 * TK, BF16)

        def mark(jb, c):
            target = jnp.where(blk == jb, tl, -2).astype(F32).astype(BF16)
            part_ref[jb] = jnp.where(part_ref[jb] == target, row16, far16)
            return c

        lax.fori_loop(0, nblk, mark, 0)

        def row_step(s, v):
            cand = v | lax.shift_left(jnp.int32(1), 7 - s)
            cf = small_float(cand)
            return jnp.where(count16(lambda x: x < cf) < want, cand, v)

        j_ref[...] = blk * TK + lax.fori_loop(0, 8, row_step, zero_q)

    jv = j_ref[...]

    def bias_block(jb, c):
        k = key_ref[jb]
        g = jb * TK + kpos
        sel = jnp.logical_or(k > t, jnp.logical_and(k == t, g <= jv))
        vis = jnp.logical_or(jb < i, diag)
        bias_ref[jb] = jnp.where(jnp.logical_and(sel, vis), 0.0, NEG)
        return c

    lax.fori_loop(0, nblk, bias_block, 0)

    _init_state(m_ref, l_ref, acc_ref)

    def attn_block(jb, c):
        kb = k_ref[pl.ds(pl.multiple_of(jb * TK, TK), TK), :]
        vt = vt_ref[jb]
        bias = bias_ref[jb]
        s = _dot(kb, qw_ref[0])
        rep = B_HEADS // B_KV_HEADS
        for cidx in range(B_HEADS * TQ // LANES):
            cs = slice(cidx * LANES, (cidx + 1) * LANES)
            g = cidx * LANES // (rep * TQ)
            bc = (cidx * LANES) % TQ
            sc = s[:, cs] + bias[:, bc:bc + LANES]
            m_prev = m_ref[:, cs]
            m_new = jnp.maximum(m_prev, jnp.max(sc, axis=0, keepdims=True))
            alpha = jnp.exp2(m_prev - m_new)
            p = jnp.exp2(sc - m_new)
            l_ref[:, cs] = alpha * l_ref[:, cs] + jnp.sum(p, axis=0, keepdims=True)
            acc_ref[:, cs] = alpha * acc_ref[:, cs] + _dot(vt[g * B_HDIM:(g + 1) * B_HDIM], p.astype(BF16))
            m_ref[:, cs] = m_new
        return c

    lax.fori_loop(0, nblk, attn_block, 0)
    _write_wide(o_ref, acc_ref[...] / l_ref[...], B_HEADS)


def _gqa_pv(vts, pb, n_groups, hdim):
    gw = pb.shape[1] // n_groups
    outs = []
    for g in range(n_groups):
        o = None
        k0 = 0
        for vt in vts:
            kc = vt.shape[1]
            part = _dot(vt[g * hdim:(g + 1) * hdim], pb[k0:k0 + kc, g * gw:(g + 1) * gw])
            o = part if o is None else o + part
            k0 += kc
        outs.append(o)
    return jnp.concatenate(outs, axis=1)


def _write_wide(o_ref, ow, n_heads):
    for j in range(n_heads // 2):
        ot = jnp.concatenate([ow[:, 2 * j * TQ:(2 * j + 1) * TQ], ow[:, (2 * j + 1) * TQ:(2 * j + 2) * TQ]], axis=0)
        o_ref[:, j * LANES:(j + 1) * LANES] = ot.T.astype(o_ref.dtype)


def _wide_scratch(n_heads, dv):
    return [pltpu.VMEM((1, n_heads * TQ), F32), pltpu.VMEM((1, n_heads * TQ), F32),
            pltpu.VMEM((dv, n_heads * TQ), F32)]


def _dsa_attn_call(qw, k, vt, qit, ki, wt, batch, seq):
    nq = seq // TQ
    nkb = seq // TK
    n_sel = min(TOPK_MAX, seq // 4)
    qcol = lambda rows: pl.BlockSpec((rows, TQ), lambda b, i: (0, b * nq + i))
    brow = lambda wdt: pl.BlockSpec((seq, wdt), lambda b, i: (b, 0))
    ow = B_HEADS * B_HDIM
    return pl.pallas_call(
        functools.partial(_dsa_attn_kernel, n_sel),
        grid=(batch, nq),
        in_specs=[pl.BlockSpec((1,) + qw.shape[1:], lambda b, i: (b * nq + i, 0, 0)), brow(k.shape[1]),
                  pl.BlockSpec((nkb,) + vt.shape[1:], lambda b, i: (b, 0, 0)),
                  qcol(qit.shape[0]), brow(ki.shape[1]), qcol(wt.shape[0])],
        out_specs=pl.BlockSpec((TQ, ow), lambda b, i: (b * nq + i, 0)),
        out_shape=jax.ShapeDtypeStruct((batch * seq, ow), BF16),
        scratch_shapes=[pltpu.VMEM((nkb, TK, TQ), I32), pltpu.VMEM((nkb, TK, TQ), BF16),
                        pltpu.VMEM((nkb, TK, TQ), F32),
                        pltpu.VMEM((1, TQ), I32)] + _wide_scratch(B_HEADS, B_HDIM),
        compiler_params=_cparams(("parallel", "arbitrary")),
        name="dsa_attn",
    )(qw, k, vt, qit, ki, wt)


SWA_SLAB = TQ + WIN_CHUNKS * CHUNK


def _swa_attn_kernel(sink_ref, qw_ref, k_ref, vt_ref, o_ref):
    i = pl.program_id(1)
    s0 = pl.multiple_of(jnp.maximum(i * TQ - SWA_KB, 0), SWA_KB)
    kslab = k_ref[pl.ds(s0, SWA_SLAB), :]
    kc = _chunk_of(s0 + lax.broadcasted_iota(I32, (SWA_SLAB, TQ), 0))
    qc = _chunk_of(i * TQ + lax.broadcasted_iota(I32, (SWA_SLAB, TQ), 1))
    valid = jnp.logical_and(kc <= qc, kc >= qc - WIN_CHUNKS)
    b0 = lax.div(s0, SWA_KB)
    vts = [vt_ref[b0 + c] for c in range(SWA_SLAB // SWA_KB)]
    s = _dot(kslab, qw_ref[0])
    s = jnp.concatenate([jnp.where(valid, s[:, h * TQ:(h + 1) * TQ], NEG) for h in range(C_HEADS)], axis=1)
    sink = jnp.concatenate([jnp.full((1, TQ), sink_ref[h] * LOG2E, F32) for h in range(C_HEADS)], axis=1)
    m = jnp.maximum(jnp.max(s, axis=0, keepdims=True), sink)
    p = jnp.exp2(s - m)
    l = jnp.sum(p, axis=0, keepdims=True) + jnp.exp2(sink - m)
    _write_wide(o_ref, _gqa_pv(vts, p.astype(BF16), C_KV_HEADS, C_HDIM) / l, C_HEADS)


def _swa_attn_call(sinks, qw, k, vt, batch, seq):
    nq = seq // TQ
    nkb = seq // SWA_KB
    ow = C_HEADS * C_HDIM
    return pl.pallas_call(
        _swa_attn_kernel,
        grid=(batch, nq),
        in_specs=[pl.BlockSpec(memory_space=pltpu.SMEM),
                  pl.BlockSpec((1,) + qw.shape[1:], lambda b, i: (b * nq + i, 0, 0)),
                  pl.BlockSpec((seq, k.shape[1]), lambda b, i: (b, 0)),
                  pl.BlockSpec((nkb,) + vt.shape[1:], lambda b, i: (b, 0, 0))],
        out_specs=pl.BlockSpec((TQ, ow), lambda b, i: (b * nq + i, 0)),
        out_shape=jax.ShapeDtypeStruct((batch * seq, ow), BF16),
        compiler_params=_cparams(("parallel", "arbitrary")),
        name="swa_attn",
    )(sinks, qw, k, vt)


def _merge_kernel(oa_ref, ob_ref, oc_ref, ga_ref, gb_ref, gc_ref, x_ref, wa_ref, wb_ref, wc_ref, wo_ref,
                  lg_ref, lb_ref, wr_ref, rb_ref, of_ref, idx_ref, wgt_ref, cnt_ref, mask_ref, carry_ref):
    merged = (ga_ref[...].astype(F32) * _dot(oa_ref[...], wa_ref[...])
              + gb_ref[...].astype(F32) * _dot(ob_ref[...], wb_ref[...])
              + gc_ref[...].astype(F32) * _dot(oc_ref[...], wc_ref[...]))
    mix = _dot(merged.astype(BF16), wo_ref[...])
    y = _layer_norm(ALPHA * x_ref[...] + mix, lg_ref[...], lb_ref[...])
    of_ref[...] = y
    _route(y, wr_ref, rb_ref, idx_ref, wgt_ref, cnt_ref, mask_ref, carry_ref)


def _merge_call(oa, ob, oc, ga, gb, gc, x, wa, wb, wc, wo, lg, lb, wr_t, rb):
    n, d = x.shape
    row = lambda wdt: pl.BlockSpec((TM, wdt), lambda i: (i, 0))
    full = lambda a: pl.BlockSpec(a.shape, lambda i: (0, 0))
    col = pl.BlockSpec((SUBLANES, TM), lambda i: (0, i))
    return pl.pallas_call(
        _merge_kernel,
        grid=(n // TM,),
        in_specs=[row(oa.shape[1]), row(ob.shape[1]), row(oc.shape[1]), row(d), row(d), row(d), row(d),
                  full(wa), full(wb), full(wc), full(wo), full(lg), full(lb), full(wr_t), full(rb)],
        out_specs=[row(d), col, col, pl.BlockSpec((N_EXPERTS, LANES), lambda i: (0, 0))],
        out_shape=[jax.ShapeDtypeStruct((n, d), F32), jax.ShapeDtypeStruct((SUBLANES, n), I32),
                   jax.ShapeDtypeStruct((SUBLANES, n), F32), jax.ShapeDtypeStruct((N_EXPERTS, LANES), F32)],
        scratch_shapes=[pltpu.VMEM((N_EXPERTS, TM), F32), pltpu.VMEM((N_EXPERTS, 1), F32)],
        compiler_params=_cparams(("arbitrary",)),
        name="merge",
    )(oa, ob, oc, ga, gb, gc, x, wa, wb, wc, wo, lg, lb, wr_t, rb)


def _route(x, wr_ref, rb_ref, idx_ref, wgt_ref, cnt_ref, mask_ref, carry_ref):
    @pl.when(pl.program_id(0) == 0)
    def _():
        carry_ref[...] = jnp.zeros(carry_ref.shape, F32)

    logits = lax.dot_general(wr_ref[...], x, (((1,), (1,)), ((), ())),
                             precision=lax.Precision.HIGHEST, preferred_element_type=F32)
    scores = jax.nn.sigmoid(logits)
    biased = scores + rb_ref[...]
    sc = [scores[e:e + 1, :] for e in range(N_EXPERTS)]
    bi = [biased[e:e + 1, :] for e in range(N_EXPERTS)]
    epg = EXPERTS_PER_GROUP
    gs = []
    for g in range(N_GROUPS):
        b = bi[g * epg:(g + 1) * epg]
        best = None
        for a in range(epg):
            for c in range(a + 1, epg):
                pair = b[a] + b[c]
                best = pair if best is None else jnp.maximum(best, pair)
        gs.append(best)
    gsel = []
    for g in range(N_GROUPS):
        ok = None
        for o in range(N_GROUPS):
            if o == g:
                continue
            cond = gs[g] > gs[o] if o < g else gs[g] >= gs[o]
            ok = cond if ok is None else jnp.logical_and(ok, cond)
        gsel.append(ok)
    chosen = []
    for e in range(N_EXPERTS):
        g = e // epg
        rank = jnp.zeros_like(bi[e], dtype=I32)
        for o in range(g * epg, (g + 1) * epg):
            if o == e:
                continue
            ahead = bi[o] >= bi[e] if o < e else bi[o] > bi[e]
            rank = rank + ahead.astype(I32)
        chosen.append(jnp.logical_and(gsel[g], rank < TOP_K))
    picked = [jnp.where(chosen[e], sc[e], 0.0) for e in range(N_EXPERTS)]
    total = picked[0]
    for e in range(1, N_EXPERTS):
        total = total + picked[e]
    for e in range(N_EXPERTS):
        mask_ref[e:e + 1, :] = chosen[e].astype(F32)
    mask = mask_ref[...]
    tm = mask.shape[1]
    upper = (lax.broadcasted_iota(I32, (tm, tm), 0) <= lax.broadcasted_iota(I32, (tm, tm), 1)).astype(BF16)
    pos = carry_ref[...] + _dot(mask.astype(BF16), upper) - 1.0
    carry_ref[...] += jnp.sum(mask, axis=1, keepdims=True)
    cnt_ref[...] = jnp.broadcast_to(carry_ref[...], cnt_ref.shape)
    zero_i = jnp.zeros_like(bi[0], dtype=I32)
    zero_f = jnp.zeros_like(bi[0])
    seen = None
    e_lo, e_hi, p_lo, p_hi, w_lo, w_hi = zero_i, zero_i, zero_f, zero_f, zero_f, zero_f
    for e in range(N_EXPERTS):
        is_lo = chosen[e] if seen is None else jnp.logical_and(chosen[e], jnp.logical_not(seen))
        is_hi = jnp.zeros_like(chosen[e]) if seen is None else jnp.logical_and(chosen[e], seen)
        seen = chosen[e] if seen is None else jnp.logical_or(seen, chosen[e])
        pe = pos[e:e + 1, :]
        we = picked[e] / total
        e_lo = jnp.where(is_lo, e, e_lo)
        e_hi = jnp.where(is_hi, e, e_hi)
        p_lo = jnp.where(is_lo, pe, p_lo)
        p_hi = jnp.where(is_hi, pe, p_hi)
        w_lo = jnp.where(is_lo, we, w_lo)
        w_hi = jnp.where(is_hi, we, w_hi)
    idx_ref[...] = jnp.zeros(idx_ref.shape, I32)
    idx_ref[0:1, :] = e_lo
    idx_ref[1:2, :] = e_hi
    idx_ref[2:3, :] = p_lo.astype(I32)
    idx_ref[3:4, :] = p_hi.astype(I32)
    wgt_ref[...] = jnp.zeros(wgt_ref.shape, F32)
    wgt_ref[0:1, :] = w_lo
    wgt_ref[1:2, :] = w_hi


def _row_copy(src, src_row, dst, dst_row, sem):
    return pltpu.make_async_copy(src.at[pl.ds(src_row, 1)], dst.at[pl.ds(dst_row, 1)], sem)


def _dispatch_kernel(dlo_ref, dhi_ref, pad_ref, x_ref, xs_hbm, zero_ref, sem):
    i = pl.program_id(0)

    @pl.when(i == 0)
    def _():
        zero_ref[...] = jnp.zeros(zero_ref.shape, F32)
        for j in range(2 * N_EXPERTS):
            @pl.when(pad_ref[j] >= 0)
            def _():
                first = pl.multiple_of(pad_ref[j], TM_MOE)
                cp = pltpu.make_async_copy(zero_ref, xs_hbm.at[pl.ds(first, TM_MOE)], sem.at[2])
                cp.start()
                cp.wait()

    base = i * TM

    def body(g, c):
        for u in range(ROW_UNROLL):
            r = g * ROW_UNROLL + u
            t = base + r
            _row_copy(x_ref, r, xs_hbm, dlo_ref[t], sem.at[0]).start(priority=u % 2)
            _row_copy(x_ref, r, xs_hbm, dhi_ref[t], sem.at[1]).start(priority=(u + 1) % 2)
        return c

    lax.fori_loop(0, TM // ROW_UNROLL, body, 0)
    pltpu.make_async_copy(x_ref, xs_hbm.at[pl.ds(0, TM)], sem.at[0]).wait()
    pltpu.make_async_copy(x_ref, xs_hbm.at[pl.ds(0, TM)], sem.at[1]).wait()


def _dispatch_call(dlo, dhi, pad_start, xf, n_rows):
    n, d = xf.shape
    return pl.pallas_call(
        _dispatch_kernel,
        grid_spec=pltpu.PrefetchScalarGridSpec(
            num_scalar_prefetch=3,
            grid=(n // TM,),
            in_specs=[pl.BlockSpec((TM, d), lambda i, a, b, c: (i, 0))],
            out_specs=pl.BlockSpec(memory_space=pl.ANY),
            scratch_shapes=[pltpu.VMEM((TM_MOE, d), F32), pltpu.SemaphoreType.DMA((3,))]),
        out_shape=jax.ShapeDtypeStruct((n_rows, d), F32),
        compiler_params=_cparams(("arbitrary",)),
        name="moe_dispatch",
    )(dlo, dhi, pad_start, xf)


def _experts_kernel(te_ref, nt_ref, xs_ref, wg_ref, wu_ref, wd_ref, ys_ref, wgb_ref, wub_ref, wdb_ref):
    i = pl.program_id(0)
    used = i < nt_ref[0]

    @pl.when(jnp.logical_or(i == 0, te_ref[i] != te_ref[jnp.maximum(i - 1, 0)]))
    def _():
        wgb_ref[...] = wg_ref[0].astype(BF16)
        wub_ref[...] = wu_ref[0].astype(BF16)
        wdb_ref[...] = wd_ref[0].astype(BF16)

    @pl.when(used)
    def _():
        x = xs_ref[...].astype(BF16)
        a = _dot(x, wgb_ref[...])
        u = _dot(x, wub_ref[...])
        h = a * jax.nn.sigmoid(a) * u
        ys_ref[...] = _dot(h.astype(BF16), wdb_ref[...])

    @pl.when(jnp.logical_not(used))
    def _():
        ys_ref[...] = jnp.zeros(ys_ref.shape, F32)


def _experts_call(tile_expert, n_tiles, xs, wg, wu, wd):
    n_rows, d = xs.shape
    tile = lambda i, te, nt: (jnp.maximum(jnp.minimum(i, nt[0] - 1), 0), 0)
    return pl.pallas_call(
        _experts_kernel,
        grid_spec=pltpu.PrefetchScalarGridSpec(
            num_scalar_prefetch=2,
            grid=(n_rows // TM_MOE,),
            in_specs=[pl.BlockSpec((TM_MOE, d), tile),
                      pl.BlockSpec((1, d, D_EXPERT), lambda i, te, nt: (te[i], 0, 0)),
                      pl.BlockSpec((1, d, D_EXPERT), lambda i, te, nt: (te[i], 0, 0)),
                      pl.BlockSpec((1, D_EXPERT, d), lambda i, te, nt: (te[i], 0, 0))],
            out_specs=pl.BlockSpec((TM_MOE, d), lambda i, te, nt: (i, 0)),
            scratch_shapes=[pltpu.VMEM((d, D_EXPERT), BF16), pltpu.VMEM((d, D_EXPERT), BF16),
                            pltpu.VMEM((D_EXPERT, d), BF16)]),
        out_shape=jax.ShapeDtypeStruct((n_rows, d), F32),
        compiler_params=_cparams(("arbitrary",)),
        name="moe_experts",
    )(tile_expert, n_tiles, xs, wg, wu, wd)


def _combine_kernel(dlo_ref, dhi_ref, ys_hbm, xf_ref, w_ref, lg_ref, lb_ref, of_ref, ob_ref, a_ref, b_ref, sem):
    i = pl.program_id(0)
    n = pl.num_programs(0)

    def issue(tile, slot):
        base = tile * TM_CMB

        def body(g, c):
            for u in range(ROW_UNROLL):
                r = g * ROW_UNROLL + u
                t = base + r
                _row_copy(ys_hbm, dlo_ref[t], a_ref.at[slot], r, sem.at[0, slot]).start(priority=u % 2)
                _row_copy(ys_hbm, dhi_ref[t], b_ref.at[slot], r, sem.at[1, slot]).start(priority=(u + 1) % 2)
            return c

        lax.fori_loop(0, TM_CMB // ROW_UNROLL, body, 0)

    @pl.when(i == 0)
    def _():
        issue(0, 0)

    @pl.when(i + 1 < n)
    def _():
        issue(i + 1, (i + 1) % 2)

    slot = i % 2
    pltpu.make_async_copy(ys_hbm.at[pl.ds(0, TM_CMB)], a_ref.at[slot], sem.at[0, slot]).wait()
    pltpu.make_async_copy(ys_hbm.at[pl.ds(0, TM_CMB)], b_ref.at[slot], sem.at[1, slot]).wait()
    w = w_ref[...]
    ffn = w[:, 0:1] * a_ref[slot] + w[:, 1:2] * b_ref[slot]
    y = _layer_norm(ALPHA * xf_ref[...] + ffn, lg_ref[...], lb_ref[...])
    of_ref[...] = y
    ob_ref[...] = y.astype(BF16)


def _combine_call(dlo, dhi, ys, xf, w2, lg, lb):
    n, d = xf.shape
    row = lambda wdt: pl.BlockSpec((TM_CMB, wdt), lambda i, a, b: (i, 0))
    vec = pl.BlockSpec((1, d), lambda i, a, b: (0, 0))
    return pl.pallas_call(
        _combine_kernel,
        grid_spec=pltpu.PrefetchScalarGridSpec(
            num_scalar_prefetch=2,
            grid=(n // TM_CMB,),
            in_specs=[pl.BlockSpec(memory_space=pl.ANY), row(d), row(2), vec, vec],
            out_specs=[row(d), row(d)],
            scratch_shapes=[pltpu.VMEM((2, TM_CMB, d), F32), pltpu.VMEM((2, TM_CMB, d), F32),
                            pltpu.SemaphoreType.DMA((2, 2))]),
        out_shape=[jax.ShapeDtypeStruct((n, d), F32), jax.ShapeDtypeStruct((n, d), BF16)],
        compiler_params=_cparams(("arbitrary",)),
        name="moe_combine",
    )(dlo, dhi, ys, xf, w2, lg, lb)


def _moe(xf, idx, wgt, cnt, wg, wu, wd, expert_base, lg, lb):
    n, d = xf.shape
    counts = cnt[:, 0].astype(I32)
    padded = (counts + TM_MOE - 1) // TM_MOE * TM_MOE
    ends = jnp.cumsum(padded)
    starts = ends - padded
    max_tiles = (TOP_K * n) // TM_MOE + N_EXPERTS
    n_tiles = (ends[-1] // TM_MOE).astype(I32).reshape(1)
    tile_ids = jnp.arange(max_tiles, dtype=I32)
    tile_expert = jnp.minimum(jnp.sum((tile_ids[:, None] >= (ends // TM_MOE)[None, :]).astype(I32), axis=1),
                              N_EXPERTS - 1)
    onehot = lambda e: (e[None, :] == jnp.arange(N_EXPERTS, dtype=I32)[:, None])
    dlo = jnp.sum(jnp.where(onehot(idx[0]), starts[:, None], 0), axis=0) + idx[2]
    dhi = jnp.sum(jnp.where(onehot(idx[1]), starts[:, None], 0), axis=0) + idx[3]
    tail = n_tiles[0] + jnp.arange(N_EXPERTS, dtype=I32)
    pad_start = jnp.concatenate([jnp.where(counts % TM_MOE != 0, ends - TM_MOE, -1),
                                 jnp.where(tail < max_tiles, tail * TM_MOE, -1)]).astype(I32)
    xs = _dispatch_call(dlo, dhi, pad_start, xf, max_tiles * TM_MOE)
    ys = _experts_call(tile_expert + expert_base, n_tiles, xs, wg, wu, wd)
    return _combine_call(dlo, dhi, ys, xf, wgt[:2].T, lg, lb)


def _rope_tables(seq):
    pos = jnp.arange(seq, dtype=F32)[:, None]
    lane = np.arange(LANES)

    def angles(half):
        inv = ROPE_THETA ** (-jnp.arange(half, dtype=F32) / half)
        return pos * inv[None, :]

    def lane_table(half, lo, hi):
        ang = angles(half)
        idx = (lane - lo) % half
        active = (lane >= lo) & (lane < hi)
        first = ((lane - lo) % (2 * half)) < half
        cos = jnp.where(active[None, :], jnp.cos(ang)[:, idx], 1.0)
        sin = jnp.where(active[None, :], jnp.sin(ang)[:, idx] * np.where(first, -1.0, 1.0)[None, :], 0.0)
        return cos.astype(F32), sin.astype(F32)

    def feat_table(half):
        ang = angles(half)
        return jnp.cos(ang).T.astype(F32), jnp.sin(ang).T.astype(F32)

    c64, s64 = lane_table(B_HDIM // 2, 0, LANES)
    c64t, s64t = feat_table(B_HDIM // 2)
    ck, sk = lane_table(A_ROPE // 2, 0, A_ROPE)
    cqt, sqt = feat_table(A_ROPE // 2)
    return (c64, s64, c64t, s64t), (cqt, sqt, ck, sk)


def _layer_weights(l, w_in, a_w_uq, a_w_ukv):
    d = D_MODEL
    pts = [int(p) for p in np.cumsum(SPLIT_SIZES)[:-1]]
    (a_q, a_kv, a_kpe, b_q, b_k, b_v, b_qi, b_ki, b_wi, c_q, c_k, c_v, gates) = jnp.split(w_in[l], pts, axis=1)
    zeros = lambda n: jnp.zeros((d, n), F32)
    bf = lambda a: a.astype(BF16)
    w_mla = jnp.concatenate([a_q, a_kv, a_kpe, zeros(LANES - A_ROPE)], axis=1)
    dsa_w = jnp.concatenate([b_k, b_ki, zeros(LANES - IDX_DIM)], axis=1)
    qi_t = jnp.concatenate([b_qi.reshape(d, IDX_HEADS, IDX_DIM),
                            jnp.zeros((d, IDX_HEADS, LANES - IDX_DIM), F32)], axis=2).reshape(d, -1).T
    wi_t = jnp.concatenate([b_wi * (IDX_HEADS * IDX_DIM) ** -0.5, zeros(WI_ROWS - IDX_HEADS)], axis=1).T
    dsa_wt = jnp.concatenate([(b_q * (B_HDIM ** -0.5 * LOG2E)).T, b_v.T, qi_t, wi_t], axis=0)
    swa_wt = jnp.concatenate([(c_q * (C_HDIM ** -0.5 * LOG2E)).T, c_v.T], axis=0)
    uq = a_w_uq[l].reshape(A_Q_RANK, A_HEADS, A_NOPE + A_ROPE) * ((A_NOPE + A_ROPE) ** -0.5 * LOG2E)
    wq = jnp.concatenate([uq, jnp.zeros((A_Q_RANK, A_HEADS, LANES - A_NOPE - A_ROPE), F32)], axis=2)
    ukv = a_w_ukv[l].reshape(A_KV_RANK, A_HEADS, A_NOPE + A_VDIM)
    wk = jnp.concatenate([ukv[:, :, :A_NOPE], jnp.zeros((A_KV_RANK, A_HEADS, LANES - A_NOPE), F32)], axis=2)
    wv = ukv[:, :, A_NOPE:]
    return dict(w_mla=bf(w_mla), dsa_w=bf(dsa_w), dsa_wt=bf(dsa_wt), swa_w=bf(c_k), swa_wt=bf(swa_wt),
                w_gate=bf(gates), wqt=bf(wq.reshape(A_Q_RANK, -1).T), wk=bf(wk.reshape(A_KV_RANK, -1)),
                wvt=bf(wv.reshape(A_KV_RANK, -1).T))


QW = B_HEADS * B_HDIM
DSA_TOK = ((0, LANES, True), (LANES, LANES, True))
DSA_FEAT = ((0, QW, True, "wide"), (QW, LANES, False, TK), (QW + LANES, IDX_HEADS * LANES, True, 0),
            (QW + LANES + IDX_HEADS * LANES, WI_ROWS, False, 0))
SWA_TOK = ((0, LANES, True),)
SWA_FEAT = ((0, QW, True, "wide"), (QW, LANES, False, SWA_KB))


def kernel(x, ln_in_g, ln_in_b, w_in, a_q_ln_g, a_kv_ln_g, a_w_uq, a_w_ukv, c_sinks, w_br_a, w_br_b, w_br_c,
           w_out, ln1_g, ln1_b, w_router, router_bias, w_exp_gate, w_exp_up, w_exp_down, ln2_g, ln2_b):
    batch, seq, d = x.shape
    assert d == D_MODEL and seq % TM == 0 and seq % TQ == 0 and TQ == TK
    n = batch * seq
    tabs64, mla_tabs = _rope_tables(seq)
    xf, xb = _ln_call(x.reshape(n, d), ln_in_g, ln_in_b)
    wr_t = w_router.T
    rb = router_bias.reshape(N_EXPERTS, 1)
    wg_all = w_exp_gate.reshape(DEPTH * N_EXPERTS, d, D_EXPERT)
    wu_all = w_exp_up.reshape(DEPTH * N_EXPERTS, d, D_EXPERT)
    wd_all = w_exp_down.reshape(DEPTH * N_EXPERTS, D_EXPERT, d)
    bf = lambda a: a.astype(BF16)
    vec = lambda a: a.reshape(1, -1)
    for l in range(DEPTH):
        w = _layer_weights(l, w_in, a_w_uq, a_w_ukv)
        qa, ka, va = _mla_proj_call(xb, w["w_mla"], vec(a_q_ln_g[l]), vec(a_kv_ln_g[l]), w["wqt"], w["wk"],
                                    w["wvt"], mla_tabs, seq)
        o_a = _mla_attn_call(qa, ka, va, batch, seq)
        bk, bki, bq, bv, bqi, bwi = _qkv_proj_call(xb, w["dsa_w"], w["dsa_wt"], tabs64, DSA_TOK, (BF16, BF16),
                                                   DSA_FEAT, (BF16, BF16, BF16, F32), "dsa_proj", seq)
        o_b = _dsa_attn_call(bq, bk, bv, bqi, bki, bwi, batch, seq)
        ck_, cq_, cv_ = _qkv_proj_call(xb, w["swa_w"], w["swa_wt"], tabs64, SWA_TOK, (BF16,),
                                       SWA_FEAT, (BF16, BF16), "swa_proj", seq)
        o_c = _swa_attn_call(c_sinks[l], cq_, ck_, cv_, batch, seq)
        g0, g1, g2 = _gate_proj_call(xb, w["w_gate"])
        xf, idx, wgt, cnt = _merge_call(o_a, o_b, o_c, g0, g1, g2, xf, bf(w_br_a[l]), bf(w_br_b[l]),
                                        bf(w_br_c[l]), bf(w_out[l]), vec(ln1_g[l]), vec(ln1_b[l]), wr_t, rb)
        xf, xb = _moe(xf, idx, wgt, cnt, wg_all, wu_all, wd_all, l * N_EXPERTS, vec(ln2_g[l]), vec(ln2_b[l]))
    return xf.reshape(batch, seq, d)
```

```python
import functools

import jax
import jax.numpy as jnp
import numpy as np
from jax import lax
from jax.experimental import pallas as pl
from jax.experimental.pallas import tpu as pltpu

F32 = jnp.float32
BF16 = jnp.bfloat16
I32 = jnp.int32

D_MODEL = 1024
DEPTH = 2
CHUNK = 64
ROPE_THETA = 10000.0
LN_EPS = 1e-5
RMS_EPS = 1e-6
NEG = -1e30

A_HEADS, A_NOPE, A_ROPE, A_VDIM, A_Q_RANK, A_KV_RANK = 8, 64, 32, 64, 384, 256
B_HEADS, B_KV_HEADS, B_HDIM, IDX_HEADS, IDX_DIM, TOPK_MAX = 8, 2, 64, 4, 64, 256
C_HEADS, C_KV_HEADS, C_HDIM, WINDOW = 8, 2, 64, 128
WIN_CHUNKS = WINDOW // CHUNK
N_EXPERTS, N_GROUPS, TOP_K, D_EXPERT = 16, 4, 2, 512
EXPERTS_PER_GROUP = N_EXPERTS // N_GROUPS
ALPHA = (2.0 * DEPTH) ** 0.25
LOG2E = 1.4426950408889634

SPLIT_SIZES = (
    A_Q_RANK, A_KV_RANK, A_ROPE,
    B_HEADS * B_HDIM, B_KV_HEADS * B_HDIM, B_KV_HEADS * B_HDIM,
    IDX_HEADS * IDX_DIM, IDX_DIM, IDX_HEADS,
    C_HEADS * C_HDIM, C_KV_HEADS * C_HDIM, C_KV_HEADS * C_HDIM,
    3 * D_MODEL,
)

LANES = 128
SUBLANES = 8
TM = 512
TQ = 256
TK = 256
SWA_KB = WIN_CHUNKS * CHUNK
TM_MOE = 512
TM_CMB = 256
WI_ROWS = 16
VMEM_LIMIT = 56 * 1024 * 1024
F32_TINY = 1.1754943508222875e-38


def _cparams(sem):
    return pltpu.CompilerParams(dimension_semantics=sem, vmem_limit_bytes=VMEM_LIMIT)


def _dot(a, b):
    return jnp.dot(a, b, preferred_element_type=F32)


def _dot_nt(a, b):
    return lax.dot_general(a, b, (((1,), (1,)), ((), ())), preferred_element_type=F32)


def _layer_norm(z, g, b):
    mu = jnp.mean(z, axis=-1, keepdims=True)
    zc = z - mu
    var = jnp.mean(zc * zc, axis=-1, keepdims=True)
    return zc * lax.rsqrt(var + LN_EPS) * g + b


def _ln_kernel(x_ref, g_ref, b_ref, of_ref, ob_ref):
    y = _layer_norm(x_ref[...], g_ref[...], b_ref[...])
    of_ref[...] = y
    ob_ref[...] = y.astype(BF16)


def _ln_call(x, g, b):
    n, d = x.shape
    row = pl.BlockSpec((TM, d), lambda i: (i, 0))
    vec = pl.BlockSpec((1, d), lambda i: (0, 0))
    return pl.pallas_call(
        _ln_kernel,
        grid=(n // TM,),
        in_specs=[row, vec, vec],
        out_specs=[row, row],
        out_shape=[jax.ShapeDtypeStruct((n, d), F32), jax.ShapeDtypeStruct((n, d), BF16)],
        compiler_params=_cparams(("parallel",)),
        name="ln_in",
    )(x, g.reshape(1, d), b.reshape(1, d))


def _rope_lanes(y, cos, sin, lane, half):
    sw = jnp.where((lane & (2 * half - 1)) < half, pltpu.roll(y, LANES - half, 1), pltpu.roll(y, half, 1))
    return y * cos + sw * sin


def _qkv_proj_kernel(tok_segs, feat_segs, x_ref, w_ref, wt_ref, cos_ref, sin_ref, cost_ref, sint_ref, *out_refs):
    x = x_ref[...]
    n_tok = len(tok_segs)
    cos = cos_ref[...]
    sin = sin_ref[...]
    lane = lax.broadcasted_iota(I32, cos.shape, 1)
    for (start, width, rope), o_ref in zip(tok_segs, out_refs[:n_tok]):
        y = _dot(x, w_ref[:, start:start + width])
        for c in range(width // LANES):
            yc = y[:, c * LANES:(c + 1) * LANES]
            if rope:
                yc = _rope_lanes(yc, cos, sin, lane, B_HDIM // 2)
            o_ref[:, c * LANES:(c + 1) * LANES] = yc.astype(o_ref.dtype)
    ct = cost_ref[...]
    st = sint_ref[...]
    half = B_HDIM // 2
    for (start, rows, rope, kb), o_ref in zip(feat_segs, out_refs[n_tok:]):
        yt = _dot_nt(wt_ref[start:start + rows, :], x)
        if kb == "wide":
            n_heads = rows // B_HDIM
            rep = n_heads // (LANES // B_HDIM)
            zero = jnp.zeros((B_HDIM, TQ), o_ref.dtype)
            for h in range(n_heads):
                x0 = yt[h * B_HDIM:h * B_HDIM + half]
                x1 = yt[h * B_HDIM + half:(h + 1) * B_HDIM]
                r0 = (x0 * ct - x1 * st).astype(o_ref.dtype)
                r1 = (x1 * ct + x0 * st).astype(o_ref.dtype)
                g = h // rep
                for c in range(yt.shape[1] // TQ):
                    cs = slice(c * TQ, (c + 1) * TQ)
                    hs = slice(h * TQ, (h + 1) * TQ)
                    o_ref[c, g * B_HDIM:g * B_HDIM + half, hs] = r0[:, cs]
                    o_ref[c, g * B_HDIM + half:(g + 1) * B_HDIM, hs] = r1[:, cs]
                    o_ref[c, (1 - g) * B_HDIM:(2 - g) * B_HDIM, hs] = zero
        elif rope:
            for r in range(rows // B_HDIM):
                x0 = yt[r * B_HDIM:r * B_HDIM + half]
                x1 = yt[r * B_HDIM + half:(r + 1) * B_HDIM]
                o_ref[r * B_HDIM:r * B_HDIM + half, :] = (x0 * ct - x1 * st).astype(o_ref.dtype)
                o_ref[r * B_HDIM + half:(r + 1) * B_HDIM, :] = (x1 * ct + x0 * st).astype(o_ref.dtype)
        elif kb:
            for c in range(yt.shape[1] // kb):
                o_ref[c] = yt[:, c * kb:(c + 1) * kb].astype(o_ref.dtype)
        else:
            o_ref[...] = yt.astype(o_ref.dtype)


def _qkv_proj_call(xb, w, wt, tabs, tok_segs, tok_dtypes, feat_segs, feat_dtypes, name, seq):
    n, d = xb.shape
    tpb = seq // TM
    cos, sin, cost, sint = tabs
    full = lambda a: pl.BlockSpec(a.shape, lambda i: (0, 0))
    out_specs, out_shape = [], []
    for (start, width, rope), dt in zip(tok_segs, tok_dtypes):
        out_specs.append(pl.BlockSpec((TM, width), lambda i: (i, 0)))
        out_shape.append(jax.ShapeDtypeStruct((n, width), dt))
    for (start, rows, rope, kb), dt in zip(feat_segs, feat_dtypes):
        if kb == "wide":
            wide = (rows // B_HDIM) * TQ
            out_specs.append(pl.BlockSpec((TM // TQ, LANES, wide), lambda i: (i, 0, 0)))
            out_shape.append(jax.ShapeDtypeStruct((n // TQ, LANES, wide), dt))
        elif kb:
            out_specs.append(pl.BlockSpec((TM // kb, rows, kb), lambda i: (i, 0, 0)))
            out_shape.append(jax.ShapeDtypeStruct((n // kb, rows, kb), dt))
        else:
            out_specs.append(pl.BlockSpec((rows, TM), lambda i: (0, i)))
            out_shape.append(jax.ShapeDtypeStruct((rows, n), dt))
    return pl.pallas_call(
        functools.partial(_qkv_proj_kernel, tok_segs, feat_segs),
        grid=(n // TM,),
        in_specs=[pl.BlockSpec((TM, d), lambda i: (i, 0)), full(w), full(wt),
                  pl.BlockSpec((TM, LANES), lambda i: (i % tpb, 0)),
                  pl.BlockSpec((TM, LANES), lambda i: (i % tpb, 0)),
                  pl.BlockSpec((B_HDIM // 2, TM), lambda i: (0, i % tpb)),
                  pl.BlockSpec((B_HDIM // 2, TM), lambda i: (0, i % tpb))],
        out_specs=out_specs,
        out_shape=out_shape,
        compiler_params=_cparams(("parallel",)),
        name=name,
    )(xb, w, wt, cos, sin, cost, sint)


def _gate_proj_kernel(x_ref, w_ref, *out_refs):
    x = x_ref[...]
    for c, o_ref in enumerate(out_refs):
        y = _dot(x, w_ref[:, c * D_MODEL:(c + 1) * D_MODEL])
        o_ref[...] = jax.nn.sigmoid(y).astype(o_ref.dtype)


def _gate_proj_call(xb, w):
    n, d = xb.shape
    row = pl.BlockSpec((TM, d), lambda i: (i, 0))
    return pl.pallas_call(
        _gate_proj_kernel,
        grid=(n // TM,),
        in_specs=[row, pl.BlockSpec(w.shape, lambda i: (0, 0))],
        out_specs=[row] * 3,
        out_shape=[jax.ShapeDtypeStruct((n, d), BF16)] * 3,
        compiler_params=_cparams(("parallel",)),
        name="gate_proj",
    )(xb, w)


def _mla_proj_kernel(x_ref, w1_ref, qg_ref, kvg_ref, wqt_ref, wk_ref, wvt_ref,
                     cqt_ref, sqt_ref, ck_ref, sk_ref, qt_ref, k_ref, vt_ref):
    y = _dot(x_ref[...], w1_ref[...])
    ql = y[:, :A_Q_RANK]
    kvl = y[:, A_Q_RANK:A_Q_RANK + A_KV_RANK]
    kpe = y[:, A_Q_RANK + A_KV_RANK:]
    qn = (ql * lax.rsqrt(jnp.mean(ql * ql, axis=-1, keepdims=True) + RMS_EPS) * qg_ref[...]).astype(BF16)
    kvn = (kvl * lax.rsqrt(jnp.mean(kvl * kvl, axis=-1, keepdims=True) + RMS_EPS) * kvg_ref[...]).astype(BF16)
    half = A_ROPE // 2
    qt = _dot_nt(wqt_ref[...], qn)
    ct = cqt_ref[...]
    st = sqt_ref[...]
    for h in range(A_HEADS):
        base = h * LANES
        r0, r1, r2 = base + A_NOPE, base + A_NOPE + half, base + A_NOPE + A_ROPE
        x0 = qt[r0:r1]
        x1 = qt[r1:r2]
        qt_ref[base:r0, :] = qt[base:r0].astype(BF16)
        qt_ref[r0:r1, :] = (x0 * ct - x1 * st).astype(BF16)
        qt_ref[r1:r2, :] = (x1 * ct + x0 * st).astype(BF16)
        qt_ref[r2:base + LANES, :] = qt[r2:base + LANES].astype(BF16)
    kn = _dot(kvn, wk_ref[...])
    lane = lax.broadcasted_iota(I32, kpe.shape, 1)
    kpe_sw = jnp.where(lane < half, pltpu.roll(kpe, LANES - half, 1), pltpu.roll(kpe, half, 1))
    kpe_p = pltpu.roll(kpe * ck_ref[...] + kpe_sw * sk_ref[...], A_NOPE, 1)
    for h in range(A_HEADS):
        sl = slice(h * LANES, (h + 1) * LANES)
        k_ref[:, sl] = (kn[:, sl] + kpe_p).astype(BF16)
    vt = _dot_nt(wvt_ref[...], kvn)
    for c in range(vt.shape[1] // TK):
        vt_ref[c] = vt[:, c * TK:(c + 1) * TK].astype(BF16)


def _mla_proj_call(xb, w1, qg, kvg, wqt, wk, wvt, tabs, seq):
    n, d = xb.shape
    tpb = seq // TM
    cqt, sqt, ck, sk = tabs
    full = lambda a: pl.BlockSpec(a.shape, lambda i: (0, 0))
    ttab = pl.BlockSpec((A_ROPE // 2, TM), lambda i: (0, i % tpb))
    ltab = pl.BlockSpec((TM, LANES), lambda i: (i % tpb, 0))
    hw = A_HEADS * LANES
    vw = A_HEADS * A_VDIM
    return pl.pallas_call(
        _mla_proj_kernel,
        grid=(n // TM,),
        in_specs=[pl.BlockSpec((TM, d), lambda i: (i, 0)), full(w1), full(qg), full(kvg), full(wqt), full(wk),
                  full(wvt), ttab, ttab, ltab, ltab],
        out_specs=[pl.BlockSpec((hw, TM), lambda i: (0, i)),
                   pl.BlockSpec((TM, hw), lambda i: (i, 0)),
                   pl.BlockSpec((TM // TK, vw, TK), lambda i: (i, 0, 0))],
        out_shape=[jax.ShapeDtypeStruct((hw, n), BF16), jax.ShapeDtypeStruct((n, hw), BF16),
                   jax.ShapeDtypeStruct((n // TK, vw, TK), BF16)],
        compiler_params=_cparams(("parallel",)),
        name="mla_proj",
    )(xb, w1, qg, kvg, wqt, wk, wvt, cqt, sqt, ck, sk)


def _chunk_of(pos):
    return lax.shift_right_logical(pos, 6)


def _diag_visible():
    k = lax.broadcasted_iota(I32, (TK, TQ), 0)
    q = lax.broadcasted_iota(I32, (TK, TQ), 1)
    return _chunk_of(k) <= _chunk_of(q)


def _online_step(h, s, vt, m_ref, l_ref, acc_ref):
    for c in range(s.shape[1] // LANES):
        cs = slice(c * LANES, (c + 1) * LANES)
        sc = s[:, cs]
        m_prev = m_ref[h, :, cs]
        m_new = jnp.maximum(m_prev, jnp.max(sc, axis=0, keepdims=True))
        alpha = jnp.exp2(m_prev - m_new)
        p = jnp.exp2(sc - m_new)
        l_ref[h, :, cs] = alpha * l_ref[h, :, cs] + jnp.sum(p, axis=0, keepdims=True)
        acc_ref[h, :, cs] = alpha * acc_ref[h, :, cs] + _dot(vt, p.astype(BF16))
        m_ref[h, :, cs] = m_new


def _init_state(m_ref, l_ref, acc_ref):
    m_ref[...] = jnp.full(m_ref.shape, NEG, F32)
    l_ref[...] = jnp.zeros(l_ref.shape, F32)
    acc_ref[...] = jnp.zeros(acc_ref.shape, F32)


def _write_heads(o_ref, l_ref, acc_ref, n_heads):
    for j in range(n_heads // 2):
        ot = jnp.concatenate([acc_ref[2 * j] / l_ref[2 * j], acc_ref[2 * j + 1] / l_ref[2 * j + 1]], axis=0)
        o_ref[:, j * LANES:(j + 1) * LANES] = ot.T.astype(o_ref.dtype)


def _mla_attn_kernel(qt_ref, k_ref, vt_ref, o_ref, m_ref, l_ref, acc_ref):
    i = pl.program_id(1)
    _init_state(m_ref, l_ref, acc_ref)
    diag = _diag_visible()

    def block(j, masked):
        rows = pl.ds(pl.multiple_of(j * TK, TK), TK)
        vt = vt_ref[j]
        ss = [_dot(k_ref[rows, h * LANES:(h + 1) * LANES], qt_ref[h * LANES:(h + 1) * LANES, :])
              for h in range(A_HEADS)]
        for h in range(A_HEADS):
            s = jnp.where(diag, ss[h], NEG) if masked else ss[h]
            _online_step(h, s, vt[h * A_VDIM:(h + 1) * A_VDIM], m_ref, l_ref, acc_ref)

    def body(j, c):
        block(j, False)
        return c

    lax.fori_loop(0, i, body, 0)
    block(i, True)
    _write_heads(o_ref, l_ref, acc_ref, A_HEADS)


def _attn_scratch(n_heads, dv):
    return [pltpu.VMEM((n_heads, 1, TQ), F32), pltpu.VMEM((n_heads, 1, TQ), F32),
            pltpu.VMEM((n_heads, dv, TQ), F32)]


def _mla_attn_call(qt, k, vt, batch, seq):
    nq = seq // TQ
    nkb = seq // TK
    hw = A_HEADS * LANES
    vw = A_HEADS * A_VDIM
    return pl.pallas_call(
        _mla_attn_kernel,
        grid=(batch, nq),
        in_specs=[pl.BlockSpec((hw, TQ), lambda b, i: (0, b * nq + i)),
                  pl.BlockSpec((seq, hw), lambda b, i: (b, 0)),
                  pl.BlockSpec((nkb, vw, TK), lambda b, i: (b, 0, 0))],
        out_specs=pl.BlockSpec((TQ, vw), lambda b, i: (b * nq + i, 0)),
        out_shape=jax.ShapeDtypeStruct((batch * seq, vw), BF16),
        scratch_shapes=_attn_scratch(A_HEADS, A_VDIM),
        compiler_params=_cparams(("parallel", "arbitrary")),
        name="mla_attn",
    )(qt, k, vt)


def _dsa_attn_kernel(n_sel, qw_ref, k_ref, vt_ref, qit_ref, ki_ref, wt_ref, o_ref,
                     key_ref, part_ref, bias_ref, j_ref, m_ref, l_ref, acc_ref):
    i = pl.program_id(1)
    nblk = i + 1
    diag = _diag_visible()
    kpos = lax.broadcasted_iota(I32, (TK, TQ), 0)
    w = wt_ref[...]

    def score_block(jb, c):
        ki = ki_ref[pl.ds(pl.multiple_of(jb * TK, TK), TK), :]
        sc = jnp.zeros((TK, TQ), F32)
        for h in range(IDX_HEADS):
            rel = jnp.maximum(_dot(ki, qit_ref[h * LANES:(h + 1) * LANES, :]), 0.0)
            sc = sc + w[h:h + 1, :] * rel
        sc = jnp.where(jnp.logical_or(jb < i, diag), sc, NEG)
        sc = jnp.where(jnp.abs(sc) < F32_TINY, 0.0, sc)
        bits = lax.bitcast_convert_type(sc, I32)
        key_ref[jb] = jnp.where(bits < 0, bits ^ 0x7FFFFFFF, bits)
        part_ref[jb] = lax.bitcast_convert_type(bits & -65536, F32).astype(BF16)
        return c

    lax.fori_loop(0, nblk, score_block, 0)

    one16 = jnp.ones((TK, TQ), BF16)
    zero16 = jnp.zeros((TK, TQ), BF16)
    rows16 = 2 * SUBLANES

    def count16(pred):
        def body(jb, acc):
            ind = jnp.where(pred(part_ref[jb]), one16, zero16).reshape(TK // rows16, rows16, TQ)
            parts = [ind[r] for r in range(TK // rows16)]
            while len(parts) > 1:
                parts = [parts[a] + parts[a + 1] for a in range(0, len(parts), 2)]
            return acc + parts[0]
        acc = lax.fori_loop(0, nblk, body, jnp.zeros((rows16, TQ), BF16))
        return jnp.sum(acc.astype(F32), axis=0, keepdims=True).astype(I32)

    def greedy(n_bits, start, to_float, want):
        def step(s, v):
            cand = v | lax.shift_left(jnp.int32(1), n_bits - 1 - s)
            cf = to_float(cand)
            c = count16(lambda x: x >= cf)
            return jnp.where(c >= want, cand, v)
        return lax.fori_loop(0, n_bits, step, start)

    def upper_float(kh):
        p = jnp.where(kh >= 0, kh, kh ^ 0x7FFF)
        p = jnp.where(jnp.logical_and(p > 0, p < 0x80), 0x80, p)
        return lax.bitcast_convert_type(lax.shift_left(p, 16), F32).astype(BF16)

    small_float = lambda v: v.astype(F32).astype(BF16)

    c0 = count16(lambda x: x >= jnp.zeros((1, TQ), BF16))
    th = greedy(15, jnp.where(c0 >= n_sel, 0, -32768).astype(I32), upper_float, n_sel)
    thf = upper_float(th)
    need_m = n_sel - count16(lambda x: x > thf)

    def mid_block(jb, c):
        k = key_ref[jb]
        band = lax.shift_right_arithmetic(k, 16) == th
        mid = (lax.shift_right_logical(k, 8) & 0xFF).astype(F32)
        part_ref[jb] = jnp.where(band, mid, -1.0).astype(BF16)
        return c

    lax.fori_loop(0, nblk, mid_block, 0)
    tm = greedy(8, jnp.zeros((1, TQ), I32), small_float, need_m)
    tmf = small_float(tm)
    need_l = need_m - count16(lambda x: x > tmf)

    t24 = lax.shift_left(th, 8) | tm

    def low_block(jb, c):
        k = key_ref[jb]
        band = lax.shift_right_arithmetic(k, 8) == t24
        part_ref[jb] = jnp.where(band, (k & 0xFF).astype(F32), -1.0).astype(BF16)
        return c

    lax.fori_loop(0, nblk, low_block, 0)
    tl = greedy(8, jnp.zeros((1, TQ), I32), small_float, need_l)
    tlf = small_float(tl)
    t = lax.shift_left(t24, 8) | tl
    need = need_l - count16(lambda x: x > tlf)
    c_eq = count16(lambda x: x == tlf)

    j_ref[...] = jnp.full(j_ref.shape, 2 ** 30, I32)

    @pl.when(jnp.max(c_eq - need) > 0)
    def _():
        def scan(jb, carry):
            cum, blk, want = carry
            ind = jnp.where(part_ref[jb] == tlf, one16, zero16).reshape(TK // rows16, rows16, TQ)
            parts = [ind[r] for r in range(TK // rows16)]
            while len(parts) > 1:
                parts = [parts[a] + parts[a + 1] for a in range(0, len(parts), 2)]
            new = cum + jnp.sum(parts[0].astype(F32), axis=0, keepdims=True).astype(I32)
            hit = jnp.logical_and(cum < need, new >= need)
            return new, jnp.where(hit, jb, blk), jnp.where(hit, need - cum, want)

        zero_q = jnp.zeros((1, TQ), I32)
        _, blk, want = lax.fori_loop(0, nblk, scan, (zero_q, zero_q, zero_q + 1))

        row16 = kpos.astype(F32).astype(BF16)
        far16 = jnp.full((TK, TQ), 2.0 * TK, BF16)

        def mark(jb, c):
            target = jnp.where(blk == jb, tl, -2).astype(F32).astype(BF16)
            part_ref[jb] = jnp.where(part_ref[jb] == target, row16, far16)
            return c

        lax.fori_loop(0, nblk, mark, 0)

        def row_step(s, v):
            cand = v | lax.shift_left(jnp.int32(1), 7 - s)
            cf = small_float(cand)
            return jnp.where(count16(lambda x: x < cf) < want, cand, v)

        j_ref[...] = blk * TK + lax.fori_loop(0, 8, row_step, zero_q)

    jv = j_ref[...]

    def bias_block(jb, c):
        k = key_ref[jb]
        g = jb * TK + kpos
        sel = jnp.logical_or(k > t, jnp.logical_and(k == t, g <= jv))
        vis = jnp.logical_or(jb < i, diag)
        bias_ref[jb] = jnp.where(jnp.logical_and(sel, vis), 0.0, NEG)
        return c

    lax.fori_loop(0, nblk, bias_block, 0)

    _init_state(m_ref, l_ref, acc_ref)

    def attn_block(jb, c):
        kb = k_ref[pl.ds(pl.multiple_of(jb * TK, TK), TK), :]
        vt = vt_ref[jb]
        bias = bias_ref[jb]
        s = _dot(kb, qw_ref[0])
        rep = B_HEADS // B_KV_HEADS
        for cidx in range(B_HEADS * TQ // LANES):
            cs = slice(cidx * LANES, (cidx + 1) * LANES)
            g = cidx * LANES // (rep * TQ)
            bc = (cidx * LANES) % TQ
            sc = s[:, cs] + bias[:, bc:bc + LANES]
            m_prev = m_ref[:, cs]
            m_new = jnp.maximum(m_prev, jnp.max(sc, axis=0, keepdims=True))
            alpha = jnp.exp2(m_prev - m_new)
            p = jnp.exp2(sc - m_new)
            l_ref[:, cs] = alpha * l_ref[:, cs] + jnp.sum(p, axis=0, keepdims=True)
            acc_ref[:, cs] = alpha * acc_ref[:, cs] + _dot(vt[g * B_HDIM:(g + 1) * B_HDIM], p.astype(BF16))
            m_ref[:, cs] = m_new
        return c

    lax.fori_loop(0, nblk, attn_block, 0)
    _write_wide(o_ref, acc_ref[...] / l_ref[...], B_HEADS)


def _gqa_pv(vts, pb, n_groups, hdim):
    gw = pb.shape[1] // n_groups
    outs = []
    for g in range(n_groups):
        o = None
        k0 = 0
        for vt in vts:
            kc = vt.shape[1]
            part = _dot(vt[g * hdim:(g + 1) * hdim], pb[k0:k0 + kc, g * gw:(g + 1) * gw])
            o = part if o is None else o + part
            k0 += kc
        outs.append(o)
    return jnp.concatenate(outs, axis=1)


def _write_wide(o_ref, ow, n_heads):
    for j in range(n_heads // 2):
        ot = jnp.concatenate([ow[:, 2 * j * TQ:(2 * j + 1) * TQ], ow[:, (2 * j + 1) * TQ:(2 * j + 2) * TQ]], axis=0)
        o_ref[:, j * LANES:(j + 1) * LANES] = ot.T.astype(o_ref.dtype)


def _wide_scratch(n_heads, dv):
    return [pltpu.VMEM((1, n_heads * TQ), F32), pltpu.VMEM((1, n_heads * TQ), F32),
            pltpu.VMEM((dv, n_heads * TQ), F32)]


def _dsa_attn_call(qw, k, vt, qit, ki, wt, batch, seq):
    nq = seq // TQ
    nkb = seq // TK
    n_sel = min(TOPK_MAX, seq // 4)
    qcol = lambda rows: pl.BlockSpec((rows, TQ), lambda b, i: (0, b * nq + i))
    brow = lambda wdt: pl.BlockSpec((seq, wdt), lambda b, i: (b, 0))
    ow = B_HEADS * B_HDIM
    return pl.pallas_call(
        functools.partial(_dsa_attn_kernel, n_sel),
        grid=(batch, nq),
        in_specs=[pl.BlockSpec((1,) + qw.shape[1:], lambda b, i: (b * nq + i, 0, 0)), brow(k.shape[1]),
                  pl.BlockSpec((nkb,) + vt.shape[1:], lambda b, i: (b, 0, 0)),
                  qcol(qit.shape[0]), brow(ki.shape[1]), qcol(wt.shape[0])],
        out_specs=pl.BlockSpec((TQ, ow), lambda b, i: (b * nq + i, 0)),
        out_shape=jax.ShapeDtypeStruct((batch * seq, ow), BF16),
        scratch_shapes=[pltpu.VMEM((nkb, TK, TQ), I32), pltpu.VMEM((nkb, TK, TQ), BF16),
                        pltpu.VMEM((nkb, TK, TQ), F32),
                        pltpu.VMEM((1, TQ), I32)] + _wide_scratch(B_HEADS, B_HDIM),
        compiler_params=_cparams(("parallel", "arbitrary")),
        name="dsa_attn",
    )(qw, k, vt, qit, ki, wt)


SWA_SLAB = TQ + WIN_CHUNKS * CHUNK


def _swa_attn_kernel(sink_ref, qw_ref, k_ref, vt_ref, o_ref):
    i = pl.program_id(1)
    s0 = pl.multiple_of(jnp.maximum(i * TQ - SWA_KB, 0), SWA_KB)
    kslab = k_ref[pl.ds(s0, SWA_SLAB), :]
    kc = _chunk_of(s0 + lax.broadcasted_iota(I32, (SWA_SLAB, TQ), 0))
    qc = _chunk_of(i * TQ + lax.broadcasted_iota(I32, (SWA_SLAB, TQ), 1))
    valid = jnp.logical_and(kc <= qc, kc >= qc - WIN_CHUNKS)
    b0 = lax.div(s0, SWA_KB)
    vts = [vt_ref[b0 + c] for c in range(SWA_SLAB // SWA_KB)]
    s = _dot(kslab, qw_ref[0])
    s = jnp.concatenate([jnp.where(valid, s[:, h * TQ:(h + 1) * TQ], NEG) for h in range(C_HEADS)], axis=1)
    sink = jnp.concatenate([jnp.full((1, TQ), sink_ref[h] * LOG2E, F32) for h in range(C_HEADS)], axis=1)
    m = jnp.maximum(jnp.max(s, axis=0, keepdims=True), sink)
    p = jnp.exp2(s - m)
    l = jnp.sum(p, axis=0, keepdims=True) + jnp.exp2(sink - m)
    _write_wide(o_ref, _gqa_pv(vts, p.astype(BF16), C_KV_HEADS, C_HDIM) / l, C_HEADS)


def _swa_attn_call(sinks, qw, k, vt, batch, seq):
    nq = seq // TQ
    nkb = seq // SWA_KB
    ow = C_HEADS * C_HDIM
    return pl.pallas_call(
        _swa_attn_kernel,
        grid=(batch, nq),
        in_specs=[pl.BlockSpec(memory_space=pltpu.SMEM),
                  pl.BlockSpec((1,) + qw.shape[1:], lambda b, i: (b * nq + i, 0, 0)),
                  pl.BlockSpec((seq, k.shape[1]), lambda b, i: (b, 0)),
                  pl.BlockSpec((nkb,) + vt.shape[1:], lambda b, i: (b, 0, 0))],
        out_specs=pl.BlockSpec((TQ, ow), lambda b, i: (b * nq + i, 0)),
        out_shape=jax.ShapeDtypeStruct((batch * seq, ow), BF16),
        compiler_params=_cparams(("parallel", "arbitrary")),
        name="swa_attn",
    )(sinks, qw, k, vt)


def _merge_kernel(oa_ref, ob_ref, oc_ref, ga_ref, gb_ref, gc_ref, x_ref, wa_ref, wb_ref, wc_ref, wo_ref,
                  lg_ref, lb_ref, wr_ref, rb_ref, of_ref, idx_ref, wgt_ref, cnt_ref, mask_ref, carry_ref):
    merged = (ga_ref[...].astype(F32) * _dot(oa_ref[...], wa_ref[...])
              + gb_ref[...].astype(F32) * _dot(ob_ref[...], wb_ref[...])
              + gc_ref[...].astype(F32) * _dot(oc_ref[...], wc_ref[...]))
    mix = _dot(merged.astype(BF16), wo_ref[...])
    y = _layer_norm(ALPHA * x_ref[...] + mix, lg_ref[...], lb_ref[...])
    of_ref[...] = y
    _route(y, wr_ref, rb_ref, idx_ref, wgt_ref, cnt_ref, mask_ref, carry_ref)


def _merge_call(oa, ob, oc, ga, gb, gc, x, wa, wb, wc, wo, lg, lb, wr_t, rb):
    n, d = x.shape
    row = lambda wdt: pl.BlockSpec((TM, wdt), lambda i: (i, 0))
    full = lambda a: pl.BlockSpec(a.shape, lambda i: (0, 0))
    col = pl.BlockSpec((SUBLANES, TM), lambda i: (0, i))
    return pl.pallas_call(
        _merge_kernel,
        grid=(n // TM,),
        in_specs=[row(oa.shape[1]), row(ob.shape[1]), row(oc.shape[1]), row(d), row(d), row(d), row(d),
                  full(wa), full(wb), full(wc), full(wo), full(lg), full(lb), full(wr_t), full(rb)],
        out_specs=[row(d), col, col, pl.BlockSpec((N_EXPERTS, LANES), lambda i: (0, 0))],
        out_shape=[jax.ShapeDtypeStruct((n, d), F32), jax.ShapeDtypeStruct((SUBLANES, n), I32),
                   jax.ShapeDtypeStruct((SUBLANES, n), F32), jax.ShapeDtypeStruct((N_EXPERTS, LANES), F32)],
        scratch_shapes=[pltpu.VMEM((N_EXPERTS, TM), F32), pltpu.VMEM((N_EXPERTS, 1), F32)],
        compiler_params=_cparams(("arbitrary",)),
        name="merge",
    )(oa, ob, oc, ga, gb, gc, x, wa, wb, wc, wo, lg, lb, wr_t, rb)


def _route(x, wr_ref, rb_ref, idx_ref, wgt_ref, cnt_ref, mask_ref, carry_ref):
    @pl.when(pl.program_id(0) == 0)
    def _():
        carry_ref[...] = jnp.zeros(carry_ref.shape, F32)

    logits = lax.dot_general(wr_ref[...], x, (((1,), (1,)), ((), ())),
                             precision=lax.Precision.HIGHEST, preferred_element_type=F32)
    scores = jax.nn.sigmoid(logits)
    biased = scores + rb_ref[...]
    sc = [scores[e:e + 1, :] for e in range(N_EXPERTS)]
    bi = [biased[e:e + 1, :] for e in range(N_EXPERTS)]
    epg = EXPERTS_PER_GROUP
    gs = []
    for g in range(N_GROUPS):
        b = bi[g * epg:(g + 1) * epg]
        best = None
        for a in range(epg):
            for c in range(a + 1, epg):
                pair = b[a] + b[c]
                best = pair if best is None else jnp.maximum(best, pair)
        gs.append(best)
    gsel = []
    for g in range(N_GROUPS):
        ok = None
        for o in range(N_GROUPS):
            if o == g:
                continue
            cond = gs[g] > gs[o] if o < g else gs[g] >= gs[o]
            ok = cond if ok is None else jnp.logical_and(ok, cond)
        gsel.append(ok)
    chosen = []
    for e in range(N_EXPERTS):
        g = e // epg
        rank = jnp.zeros_like(bi[e], dtype=I32)
        for o in range(g * epg, (g + 1) * epg):
            if o == e:
                continue
            ahead = bi[o] >= bi[e] if o < e else bi[o] > bi[e]
            rank = rank + ahead.astype(I32)
        chosen.append(jnp.logical_and(gsel[g], rank < TOP_K))
    picked = [jnp.where(chosen[e], sc[e], 0.0) for e in range(N_EXPERTS)]
    total = picked[0]
    for e in range(1, N_EXPERTS):
        total = total + picked[e]
    for e in range(N_EXPERTS):
        mask_ref[e:e + 1, :] = chosen[e].astype(F32)
    mask = mask_ref[...]
    tm = mask.shape[1]
    upper = (lax.broadcasted_iota(I32, (tm, tm), 0) <= lax.broadcasted_iota(I32, (tm, tm), 1)).astype(BF16)
    pos = carry_ref[...] + _dot(mask.astype(BF16), upper) - 1.0
    carry_ref[...] += jnp.sum(mask, axis=1, keepdims=True)
    cnt_ref[...] = jnp.broadcast_to(carry_ref[...], cnt_ref.shape)
    zero_i = jnp.zeros_like(bi[0], dtype=I32)
    zero_f = jnp.zeros_like(bi[0])
    seen = None
    e_lo, e_hi, p_lo, p_hi, w_lo, w_hi = zero_i, zero_i, zero_f, zero_f, zero_f, zero_f
    for e in range(N_EXPERTS):
        is_lo = chosen[e] if seen is None else jnp.logical_and(chosen[e], jnp.logical_not(seen))
        is_hi = jnp.zeros_like(chosen[e]) if seen is None else jnp.logical_and(chosen[e], seen)
        seen = chosen[e] if seen is None else jnp.logical_or(seen, chosen[e])
        pe = pos[e:e + 1, :]
        we = picked[e] / total
        e_lo = jnp.where(is_lo, e, e_lo)
        e_hi = jnp.where(is_hi, e, e_hi)
        p_lo = jnp.where(is_lo, pe, p_lo)
        p_hi = jnp.where(is_hi, pe, p_hi)
        w_lo = jnp.where(is_lo, we, w_lo)
        w_hi = jnp.where(is_hi, we, w_hi)
    idx_ref[...] = jnp.zeros(idx_ref.shape, I32)
    idx_ref[0:1, :] = e_lo
    idx_ref[1:2, :] = e_hi
    idx_ref[2:3, :] = p_lo.astype(I32)
    idx_ref[3:4, :] = p_hi.astype(I32)
    wgt_ref[...] = jnp.zeros(wgt_ref.shape, F32)
    wgt_ref[0:1, :] = w_lo
    wgt_ref[1:2, :] = w_hi


def _row_copy(src, src_row, dst, dst_row, sem):
    return pltpu.make_async_copy(src.at[pl.ds(src_row, 1)], dst.at[pl.ds(dst_row, 1)], sem)


def _dispatch_kernel(dlo_ref, dhi_ref, pad_ref, x_ref, xs_hbm, zero_ref, sem):
    i = pl.program_id(0)

    @pl.when(i == 0)
    def _():
        zero_ref[...] = jnp.zeros(zero_ref.shape, F32)
        for j in range(2 * N_EXPERTS):
            @pl.when(pad_ref[j] >= 0)
            def _():
                first = pl.multiple_of(pad_ref[j], TM_MOE)
                cp = pltpu.make_async_copy(zero_ref, xs_hbm.at[pl.ds(first, TM_MOE)], sem.at[2])
                cp.start()
                cp.wait()

    base = i * TM

    for r in range(TM):
        t = base + r
        _row_copy(x_ref, r, xs_hbm, dlo_ref[t], sem.at[0]).start(priority=r % 2)
        _row_copy(x_ref, r, xs_hbm, dhi_ref[t], sem.at[1]).start(priority=(r + 1) % 2)
    pltpu.make_async_copy(x_ref, xs_hbm.at[pl.ds(0, TM)], sem.at[0]).wait()
    pltpu.make_async_copy(x_ref, xs_hbm.at[pl.ds(0, TM)], sem.at[1]).wait()


def _dispatch_call(dlo, dhi, pad_start, xf, n_rows):
    n, d = xf.shape
    return pl.pallas_call(
        _dispatch_kernel,
        grid_spec=pltpu.PrefetchScalarGridSpec(
            num_scalar_prefetch=3,
            grid=(n // TM,),
            in_specs=[pl.BlockSpec((TM, d), lambda i, a, b, c: (i, 0))],
            out_specs=pl.BlockSpec(memory_space=pl.ANY),
            scratch_shapes=[pltpu.VMEM((TM_MOE, d), F32), pltpu.SemaphoreType.DMA((3,))]),
        out_shape=jax.ShapeDtypeStruct((n_rows, d), F32),
        compiler_params=_cparams(("arbitrary",)),
        name="moe_dispatch",
    )(dlo, dhi, pad_start, xf)


def _experts_kernel(te_ref, nt_ref, xs_ref, wg_ref, wu_ref, wd_ref, ys_ref, wgb_ref, wub_ref, wdb_ref):
    i = pl.program_id(0)
    used = i < nt_ref[0]

    @pl.when(jnp.logical_or(i == 0, te_ref[i] != te_ref[jnp.maximum(i - 1, 0)]))
    def _():
        wgb_ref[...] = wg_ref[0].astype(BF16)
        wub_ref[...] = wu_ref[0].astype(BF16)
        wdb_ref[...] = wd_ref[0].astype(BF16)

    @pl.when(used)
    def _():
        x = xs_ref[...].astype(BF16)
        a = _dot(x, wgb_ref[...])
        u = _dot(x, wub_ref[...])
        h = a * jax.nn.sigmoid(a) * u
        ys_ref[...] = _dot(h.astype(BF16), wdb_ref[...])

    @pl.when(jnp.logical_not(used))
    def _():
        ys_ref[...] = jnp.zeros(ys_ref.shape, F32)


def _experts_call(tile_expert, n_tiles, xs, wg, wu, wd):
    n_rows, d = xs.shape
    tile = lambda i, te, nt: (jnp.maximum(jnp.minimum(i, nt[0] - 1), 0), 0)
    return pl.pallas_call(
        _experts_kernel,
        grid_spec=pltpu.PrefetchScalarGridSpec(
            num_scalar_prefetch=2,
            grid=(n_rows // TM_MOE,),
            in_specs=[pl.BlockSpec((TM_MOE, d), tile),
                      pl.BlockSpec((1, d, D_EXPERT), lambda i, te, nt: (te[i], 0, 0)),
                      pl.BlockSpec((1, d, D_EXPERT), lambda i, te, nt: (te[i], 0, 0)),
                      pl.BlockSpec((1, D_EXPERT, d), lambda i, te, nt: (te[i], 0, 0))],
            out_specs=pl.BlockSpec((TM_MOE, d), lambda i, te, nt: (i, 0)),
            scratch_shapes=[pltpu.VMEM((d, D_EXPERT), BF16), pltpu.VMEM((d, D_EXPERT), BF16),
                            pltpu.VMEM((D_EXPERT, d), BF16)]),
        out_shape=jax.ShapeDtypeStruct((n_rows, d), F32),
        compiler_params=_cparams(("arbitrary",)),
        name="moe_experts",
    )(tile_expert, n_tiles, xs, wg, wu, wd)


def _combine_kernel(dlo_ref, dhi_ref, ys_hbm, xf_ref, w_ref, lg_ref, lb_ref, of_ref, ob_ref, a_ref, b_ref, sem):
    i = pl.program_id(0)
    n = pl.num_programs(0)

    def issue(tile, slot):
        base = tile * TM_CMB

        for r in range(TM_CMB):
            t = base + r
            _row_copy(ys_hbm, dlo_ref[t], a_ref.at[slot], r, sem.at[0, slot]).start(priority=r % 2)
            _row_copy(ys_hbm, dhi_ref[t], b_ref.at[slot], r, sem.at[1, slot]).start(priority=(r + 1) % 2)

    @pl.when(i == 0)
    def _():
        issue(0, 0)

    @pl.when(i + 1 < n)
    def _():
        issue(i + 1, (i + 1) % 2)

    slot = i % 2
    pltpu.make_async_copy(ys_hbm.at[pl.ds(0, TM_CMB)], a_ref.at[slot], sem.at[0, slot]).wait()
    pltpu.make_async_copy(ys_hbm.at[pl.ds(0, TM_CMB)], b_ref.at[slot], sem.at[1, slot]).wait()
    w = w_ref[...]
    ffn = w[:, 0:1] * a_ref[slot] + w[:, 1:2] * b_ref[slot]
    y = _layer_norm(ALPHA * xf_ref[...] + ffn, lg_ref[...], lb_ref[...])
    of_ref[...] = y
    ob_ref[...] = y.astype(BF16)


def _combine_call(dlo, dhi, ys, xf, w2, lg, lb):
    n, d = xf.shape
    row = lambda wdt: pl.BlockSpec((TM_CMB, wdt), lambda i, a, b: (i, 0))
    vec = pl.BlockSpec((1, d), lambda i, a, b: (0, 0))
    return pl.pallas_call(
        _combine_kernel,
        grid_spec=pltpu.PrefetchScalarGridSpec(
            num_scalar_prefetch=2,
            grid=(n // TM_CMB,),
            in_specs=[pl.BlockSpec(memory_space=pl.ANY), row(d), row(2), vec, vec],
            out_specs=[row(d), row(d)],
            scratch_shapes=[pltpu.VMEM((2, TM_CMB, d), F32), pltpu.VMEM((2, TM_CMB, d), F32),
                            pltpu.SemaphoreType.DMA((2, 2))]),
        out_shape=[jax.ShapeDtypeStruct((n, d), F32), jax.ShapeDtypeStruct((n, d), BF16)],
        compiler_params=_cparams(("arbitrary",)),
        name="moe_combine",
    )(dlo, dhi, ys, xf, w2, lg, lb)


def _moe(xf, idx, wgt, cnt, wg, wu, wd, expert_base, lg, lb):
    n, d = xf.shape
    counts = cnt[:, 0].astype(I32)
    padded = (counts + TM_MOE - 1) // TM_MOE * TM_MOE
    ends = jnp.cumsum(padded)
    starts = ends - padded
    max_tiles = (TOP_K * n) // TM_MOE + N_EXPERTS
    n_tiles = (ends[-1] // TM_MOE).astype(I32).reshape(1)
    tile_ids = jnp.arange(max_tiles, dtype=I32)
    tile_expert = jnp.minimum(jnp.sum((tile_ids[:, None] >= (ends // TM_MOE)[None, :]).astype(I32), axis=1),
                              N_EXPERTS - 1)
    onehot = lambda e: (e[None, :] == jnp.arange(N_EXPERTS, dtype=I32)[:, None])
    dlo = jnp.sum(jnp.where(onehot(idx[0]), starts[:, None], 0), axis=0) + idx[2]
    dhi = jnp.sum(jnp.where(onehot(idx[1]), starts[:, None], 0), axis=0) + idx[3]
    tail = n_tiles[0] + jnp.arange(N_EXPERTS, dtype=I32)
    pad_start = jnp.concatenate([jnp.where(counts % TM_MOE != 0, ends - TM_MOE, -1),
                                 jnp.where(tail < max_tiles, tail * TM_MOE, -1)]).astype(I32)
    xs = _dispatch_call(dlo, dhi, pad_start, xf, max_tiles * TM_MOE)
    ys = _experts_call(tile_expert + expert_base, n_tiles, xs, wg, wu, wd)
    return _combine_call(dlo, dhi, ys, xf, wgt[:2].T, lg, lb)


def _rope_tables(seq):
    pos = jnp.arange(seq, dtype=F32)[:, None]
    lane = np.arange(LANES)

    def angles(half):
        inv = ROPE_THETA ** (-jnp.arange(half, dtype=F32) / half)
        return pos * inv[None, :]

    def lane_table(half, lo, hi):
        ang = angles(half)
        idx = (lane - lo) % half
        active = (lane >= lo) & (lane < hi)
        first = ((lane - lo) % (2 * half)) < half
        cos = jnp.where(active[None, :], jnp.cos(ang)[:, idx], 1.0)
        sin = jnp.where(active[None, :], jnp.sin(ang)[:, idx] * np.where(first, -1.0, 1.0)[None, :], 0.0)
        return cos.astype(F32), sin.astype(F32)

    def feat_table(half):
        ang = angles(half)
        return jnp.cos(ang).T.astype(F32), jnp.sin(ang).T.astype(F32)

    c64, s64 = lane_table(B_HDIM // 2, 0, LANES)
    c64t, s64t = feat_table(B_HDIM // 2)
    ck, sk = lane_table(A_ROPE // 2, 0, A_ROPE)
    cqt, sqt = feat_table(A_ROPE // 2)
    return (c64, s64, c64t, s64t), (cqt, sqt, ck, sk)


def _layer_weights(l, w_in, a_w_uq, a_w_ukv):
    d = D_MODEL
    pts = [int(p) for p in np.cumsum(SPLIT_SIZES)[:-1]]
    (a_q, a_kv, a_kpe, b_q, b_k, b_v, b_qi, b_ki, b_wi, c_q, c_k, c_v, gates) = jnp.split(w_in[l], pts, axis=1)
    bf = lambda a: a.astype(BF16)
    zeros = lambda n: jnp.zeros((d, n), BF16)
    w_mla = jnp.concatenate([bf(a_q), bf(a_kv), bf(a_kpe), zeros(LANES - A_ROPE)], axis=1)
    dsa_w = jnp.concatenate([bf(b_k), bf(b_ki), zeros(LANES - IDX_DIM)], axis=1)
    qi_t = jnp.concatenate([bf(b_qi).reshape(d, IDX_HEADS, IDX_DIM),
                            jnp.zeros((d, IDX_HEADS, LANES - IDX_DIM), BF16)], axis=2).reshape(d, -1).T
    wi_t = jnp.concatenate([bf(b_wi * (IDX_HEADS * IDX_DIM) ** -0.5), zeros(WI_ROWS - IDX_HEADS)], axis=1).T
    dsa_wt = jnp.concatenate([bf(b_q * (B_HDIM ** -0.5 * LOG2E)).T, bf(b_v).T, qi_t, wi_t], axis=0)
    swa_wt = jnp.concatenate([bf(c_q * (C_HDIM ** -0.5 * LOG2E)).T, bf(c_v).T], axis=0)
    uq = bf(a_w_uq[l].reshape(A_Q_RANK, A_HEADS, A_NOPE + A_ROPE) * ((A_NOPE + A_ROPE) ** -0.5 * LOG2E))
    wq = jnp.concatenate([uq, jnp.zeros((A_Q_RANK, A_HEADS, LANES - A_NOPE - A_ROPE), BF16)], axis=2)
    ukv = bf(a_w_ukv[l]).reshape(A_KV_RANK, A_HEADS, A_NOPE + A_VDIM)
    wk = jnp.concatenate([ukv[:, :, :A_NOPE], jnp.zeros((A_KV_RANK, A_HEADS, LANES - A_NOPE), BF16)], axis=2)
    wv = ukv[:, :, A_NOPE:]
    return dict(w_mla=w_mla, dsa_w=dsa_w, dsa_wt=dsa_wt, swa_w=bf(c_k), swa_wt=swa_wt,
                w_gate=bf(gates), wqt=wq.reshape(A_Q_RANK, -1).T, wk=wk.reshape(A_KV_RANK, -1),
                wvt=wv.reshape(A_KV_RANK, -1).T)


QW = B_HEADS * B_HDIM
DSA_TOK = ((0, LANES, True), (LANES, LANES, True))
DSA_FEAT = ((0, QW, True, "wide"), (QW, LANES, False, TK), (QW + LANES, IDX_HEADS * LANES, True, 0),
            (QW + LANES + IDX_HEADS * LANES, WI_ROWS, False, 0))
SWA_TOK = ((0, LANES, True),)
SWA_FEAT = ((0, QW, True, "wide"), (QW, LANES, False, SWA_KB))


def kernel(x, ln_in_g, ln_in_b, w_in, a_q_ln_g, a_kv_ln_g, a_w_uq, a_w_ukv, c_sinks, w_br_a, w_br_b, w_br_c,
           w_out, ln1_g, ln1_b, w_router, router_bias, w_exp_gate, w_exp_up, w_exp_down, ln2_g, ln2_b):
    batch, seq, d = x.shape
    assert d == D_MODEL and seq % TM == 0 and seq % TQ == 0 and TQ == TK
    n = batch * seq
    tabs64, mla_tabs = _rope_tables(seq)
    xf, xb = _ln_call(x.reshape(n, d), ln_in_g, ln_in_b)
    wr_t = w_router.T
    rb = router_bias.reshape(N_EXPERTS, 1)
    wg_all = w_exp_gate.reshape(DEPTH * N_EXPERTS, d, D_EXPERT)
    wu_all = w_exp_up.reshape(DEPTH * N_EXPERTS, d, D_EXPERT)
    wd_all = w_exp_down.reshape(DEPTH * N_EXPERTS, D_EXPERT, d)
    bf = lambda a: a.astype(BF16)
    vec = lambda a: a.reshape(1, -1)
    for l in range(DEPTH):
        w = _layer_weights(l, w_in, a_w_uq, a_w_ukv)
        qa, ka, va = _mla_proj_call(xb, w["w_mla"], vec(a_q_ln_g[l]), vec(a_kv_ln_g[l]), w["wqt"], w["wk"],
                                    w["wvt"], mla_tabs, seq)
        o_a = _mla_attn_call(qa, ka, va, batch, seq)
        bk, bki, bq, bv, bqi, bwi = _qkv_proj_call(xb, w["dsa_w"], w["dsa_wt"], tabs64, DSA_TOK, (BF16, BF16),
                                                   DSA_FEAT, (BF16, BF16, BF16, F32), "dsa_proj", seq)
        o_b = _dsa_attn_call(bq, bk, bv, bqi, bki, bwi, batch, seq)
        ck_, cq_, cv_ = _qkv_proj_call(xb, w["swa_w"], w["swa_wt"], tabs64, SWA_TOK, (BF16,),
                                       SWA_FEAT, (BF16, BF16), "swa_proj", seq)
        o_c = _swa_attn_call(c_sinks[l], cq_, ck_, cv_, batch, seq)
        g0, g1, g2 = _gate_proj_call(xb, w["w_gate"])
        xf, idx, wgt, cnt = _merge_call(o_a, o_b, o_c, g0, g1, g2, xf, bf(w_br_a[l]), bf(w_br_b[l]),
                                        bf(w_br_c[l]), bf(w_out[l]), vec(ln1_g[l]), vec(ln1_b[l]), wr_t, rb)
        xf, xb = _moe(xf, idx, wgt, cnt, wg_all, wu_all, wd_all, l * N_EXPERTS, vec(ln2_g[l]), vec(ln2_b[l]))
    return xf.reshape(batch, seq, d)
```

```python
import functools

import jax
import jax.numpy as jnp
import numpy as np
from jax import lax
from jax.experimental import pallas as pl
from jax.experimental.pallas import tpu as pltpu

F32 = jnp.float32
BF16 = jnp.bfloat16
I32 = jnp.int32

D_MODEL = 1024
DEPTH = 2
CHUNK = 64
ROPE_THETA = 10000.0
LN_EPS = 1e-5
RMS_EPS = 1e-6
NEG = -1e30

A_HEADS, A_NOPE, A_ROPE, A_VDIM, A_Q_RANK, A_KV_RANK = 8, 64, 32, 64, 384, 256
B_HEADS, B_KV_HEADS, B_HDIM, IDX_HEADS, IDX_DIM, TOPK_MAX = 8, 2, 64, 4, 64, 256
C_HEADS, C_KV_HEADS, C_HDIM, WINDOW = 8, 2, 64, 128
WIN_CHUNKS = WINDOW // CHUNK
N_EXPERTS, N_GROUPS, TOP_K, D_EXPERT = 16, 4, 2, 512
EXPERTS_PER_GROUP = N_EXPERTS // N_GROUPS
ALPHA = (2.0 * DEPTH) ** 0.25
LOG2E = 1.4426950408889634

SPLIT_SIZES = (
    A_Q_RANK, A_KV_RANK, A_ROPE,
    B_HEADS * B_HDIM, B_KV_HEADS * B_HDIM, B_KV_HEADS * B_HDIM,
    IDX_HEADS * IDX_DIM, IDX_DIM, IDX_HEADS,
    C_HEADS * C_HDIM, C_KV_HEADS * C_HDIM, C_KV_HEADS * C_HDIM,
    3 * D_MODEL,
)

LANES = 128
SUBLANES = 8
TM = 512
TQ = 256
TK = 256
SWA_KB = WIN_CHUNKS * CHUNK
TM_MOE = 512
TM_CMB = 256
WI_ROWS = 16
VMEM_LIMIT = 56 * 1024 * 1024
F32_TINY = 1.1754943508222875e-38


def _cparams(sem):
    return pltpu.CompilerParams(dimension_semantics=sem, vmem_limit_bytes=VMEM_LIMIT)


def _dot(a, b):
    return jnp.dot(a, b, preferred_element_type=F32)


def _dot_nt(a, b):
    return lax.dot_general(a, b, (((1,), (1,)), ((), ())), preferred_element_type=F32)


def _layer_norm(z, g, b):
    mu = jnp.mean(z, axis=-1, keepdims=True)
    zc = z - mu
    var = jnp.mean(zc * zc, axis=-1, keepdims=True)
    return zc * lax.rsqrt(var + LN_EPS) * g + b


def _ln_kernel(x_ref, g_ref, b_ref, of_ref, ob_ref):
    y = _layer_norm(x_ref[...], g_ref[...], b_ref[...])
    of_ref[...] = y
    ob_ref[...] = y.astype(BF16)


def _ln_call(x, g, b):
    n, d = x.shape
    row = pl.BlockSpec((TM, d), lambda i: (i, 0))
    vec = pl.BlockSpec((1, d), lambda i: (0, 0))
    return pl.pallas_call(
        _ln_kernel,
        grid=(n // TM,),
        in_specs=[row, vec, vec],
        out_specs=[row, row],
        out_shape=[jax.ShapeDtypeStruct((n, d), F32), jax.ShapeDtypeStruct((n, d), BF16)],
        compiler_params=_cparams(("parallel",)),
        name="ln_in",
    )(x, g.reshape(1, d), b.reshape(1, d))


def _rope_lanes(y, cos, sin, lane, half):
    sw = jnp.where((lane & (2 * half - 1)) < half, pltpu.roll(y, LANES - half, 1), pltpu.roll(y, half, 1))
    return y * cos + sw * sin


def _qkv_proj_kernel(tok_segs, feat_segs, x_ref, w_ref, wt_ref, cos_ref, sin_ref, cost_ref, sint_ref, *out_refs):
    x = x_ref[...]
    n_tok = len(tok_segs)
    cos = cos_ref[...]
    sin = sin_ref[...]
    lane = lax.broadcasted_iota(I32, cos.shape, 1)
    for (start, width, rope), o_ref in zip(tok_segs, out_refs[:n_tok]):
        y = _dot(x, w_ref[:, start:start + width])
        for c in range(width // LANES):
            yc = y[:, c * LANES:(c + 1) * LANES]
            if rope:
                yc = _rope_lanes(yc, cos, sin, lane, B_HDIM // 2)
            o_ref[:, c * LANES:(c + 1) * LANES] = yc.astype(o_ref.dtype)
    ct = cost_ref[...]
    st = sint_ref[...]
    half = B_HDIM // 2
    for (start, rows, rope, kb), o_ref in zip(feat_segs, out_refs[n_tok:]):
        yt = _dot_nt(wt_ref[start:start + rows, :], x)
        if kb == "wide":
            n_heads = rows // B_HDIM
            rep = n_heads // (LANES // B_HDIM)
            zero = jnp.zeros((B_HDIM, TQ), o_ref.dtype)
            for h in range(n_heads):
                x0 = yt[h * B_HDIM:h * B_HDIM + half]
                x1 = yt[h * B_HDIM + half:(h + 1) * B_HDIM]
                r0 = (x0 * ct - x1 * st).astype(o_ref.dtype)
                r1 = (x1 * ct + x0 * st).astype(o_ref.dtype)
                g = h // rep
                for c in range(yt.shape[1] // TQ):
                    cs = slice(c * TQ, (c + 1) * TQ)
                    hs = slice(h * TQ, (h + 1) * TQ)
                    o_ref[c, g * B_HDIM:g * B_HDIM + half, hs] = r0[:, cs]
                    o_ref[c, g * B_HDIM + half:(g + 1) * B_HDIM, hs] = r1[:, cs]
                    o_ref[c, (1 - g) * B_HDIM:(2 - g) * B_HDIM, hs] = zero
        elif rope:
            for r in range(rows // B_HDIM):
                x0 = yt[r * B_HDIM:r * B_HDIM + half]
                x1 = yt[r * B_HDIM + half:(r + 1) * B_HDIM]
                o_ref[r * B_HDIM:r * B_HDIM + half, :] = (x0 * ct - x1 * st).astype(o_ref.dtype)
                o_ref[r * B_HDIM + half:(r + 1) * B_HDIM, :] = (x1 * ct + x0 * st).astype(o_ref.dtype)
        elif kb:
            for c in range(yt.shape[1] // kb):
                o_ref[c] = yt[:, c * kb:(c + 1) * kb].astype(o_ref.dtype)
        else:
            o_ref[...] = yt.astype(o_ref.dtype)


def _qkv_proj_call(xb, w, wt, tabs, tok_segs, tok_dtypes, feat_segs, feat_dtypes, name, seq):
    n, d = xb.shape
    tpb = seq // TM
    cos, sin, cost, sint = tabs
    full = lambda a: pl.BlockSpec(a.shape, lambda i: (0, 0))
    out_specs, out_shape = [], []
    for (start, width, rope), dt in zip(tok_segs, tok_dtypes):
        out_specs.append(pl.BlockSpec((TM, width), lambda i: (i, 0)))
        out_shape.append(jax.ShapeDtypeStruct((n, width), dt))
    for (start, rows, rope, kb), dt in zip(feat_segs, feat_dtypes):
        if kb == "wide":
            wide = (rows // B_HDIM) * TQ
            out_specs.append(pl.BlockSpec((TM // TQ, LANES, wide), lambda i: (i, 0, 0)))
            out_shape.append(jax.ShapeDtypeStruct((n // TQ, LANES, wide), dt))
        elif kb:
            out_specs.append(pl.BlockSpec((TM // kb, rows, kb), lambda i: (i, 0, 0)))
            out_shape.append(jax.ShapeDtypeStruct((n // kb, rows, kb), dt))
        else:
            out_specs.append(pl.BlockSpec((rows, TM), lambda i: (0, i)))
            out_shape.append(jax.ShapeDtypeStruct((rows, n), dt))
    return pl.pallas_call(
        functools.partial(_qkv_proj_kernel, tok_segs, feat_segs),
        grid=(n // TM,),
        in_specs=[pl.BlockSpec((TM, d), lambda i: (i, 0)), full(w), full(wt),
                  pl.BlockSpec((TM, LANES), lambda i: (i % tpb, 0)),
                  pl.BlockSpec((TM, LANES), lambda i: (i % tpb, 0)),
                  pl.BlockSpec((B_HDIM // 2, TM), lambda i: (0, i % tpb)),
                  pl.BlockSpec((B_HDIM // 2, TM), lambda i: (0, i % tpb))],
        out_specs=out_specs,
        out_shape=out_shape,
        compiler_params=_cparams(("parallel",)),
        name=name,
    )(xb, w, wt, cos, sin, cost, sint)


def _gate_proj_kernel(x_ref, w_ref, *out_refs):
    x = x_ref[...]
    for c, o_ref in enumerate(out_refs):
        y = _dot(x, w_ref[:, c * D_MODEL:(c + 1) * D_MODEL])
        o_ref[...] = jax.nn.sigmoid(y).astype(o_ref.dtype)


def _gate_proj_call(xb, w):
    n, d = xb.shape
    row = pl.BlockSpec((TM, d), lambda i: (i, 0))
    return pl.pallas_call(
        _gate_proj_kernel,
        grid=(n // TM,),
        in_specs=[row, pl.BlockSpec(w.shape, lambda i: (0, 0))],
        out_specs=[row] * 3,
        out_shape=[jax.ShapeDtypeStruct((n, d), BF16)] * 3,
        compiler_params=_cparams(("parallel",)),
        name="gate_proj",
    )(xb, w)


def _mla_proj_kernel(x_ref, w1_ref, qg_ref, kvg_ref, wqt_ref, wk_ref, wvt_ref,
                     cqt_ref, sqt_ref, ck_ref, sk_ref, qt_ref, k_ref, vt_ref):
    y = _dot(x_ref[...], w1_ref[...])
    ql = y[:, :A_Q_RANK]
    kvl = y[:, A_Q_RANK:A_Q_RANK + A_KV_RANK]
    kpe = y[:, A_Q_RANK + A_KV_RANK:]
    qn = (ql * lax.rsqrt(jnp.mean(ql * ql, axis=-1, keepdims=True) + RMS_EPS) * qg_ref[...]).astype(BF16)
    kvn = (kvl * lax.rsqrt(jnp.mean(kvl * kvl, axis=-1, keepdims=True) + RMS_EPS) * kvg_ref[...]).astype(BF16)
    half = A_ROPE // 2
    qt = _dot_nt(wqt_ref[...], qn)
    ct = cqt_ref[...]
    st = sqt_ref[...]
    for h in range(A_HEADS):
        base = h * LANES
        r0, r1, r2 = base + A_NOPE, base + A_NOPE + half, base + A_NOPE + A_ROPE
        x0 = qt[r0:r1]
        x1 = qt[r1:r2]
        qt_ref[base:r0, :] = qt[base:r0].astype(BF16)
        qt_ref[r0:r1, :] = (x0 * ct - x1 * st).astype(BF16)
        qt_ref[r1:r2, :] = (x1 * ct + x0 * st).astype(BF16)
        qt_ref[r2:base + LANES, :] = qt[r2:base + LANES].astype(BF16)
    kn = _dot(kvn, wk_ref[...])
    lane = lax.broadcasted_iota(I32, kpe.shape, 1)
    kpe_sw = jnp.where(lane < half, pltpu.roll(kpe, LANES - half, 1), pltpu.roll(kpe, half, 1))
    kpe_p = pltpu.roll(kpe * ck_ref[...] + kpe_sw * sk_ref[...], A_NOPE, 1)
    for h in range(A_HEADS):
        sl = slice(h * LANES, (h + 1) * LANES)
        k_ref[:, sl] = (kn[:, sl] + kpe_p).astype(BF16)
    vt = _dot_nt(wvt_ref[...], kvn)
    for c in range(vt.shape[1] // TK):
        vt_ref[c] = vt[:, c * TK:(c + 1) * TK].astype(BF16)


def _mla_proj_call(xb, w1, qg, kvg, wqt, wk, wvt, tabs, seq):
    n, d = xb.shape
    tpb = seq // TM
    cqt, sqt, ck, sk = tabs
    full = lambda a: pl.BlockSpec(a.shape, lambda i: (0, 0))
    ttab = pl.BlockSpec((A_ROPE // 2, TM), lambda i: (0, i % tpb))
    ltab = pl.BlockSpec((TM, LANES), lambda i: (i % tpb, 0))
    hw = A_HEADS * LANES
    vw = A_HEADS * A_VDIM
    return pl.pallas_call(
        _mla_proj_kernel,
        grid=(n // TM,),
        in_specs=[pl.BlockSpec((TM, d), lambda i: (i, 0)), full(w1), full(qg), full(kvg), full(wqt), full(wk),
                  full(wvt), ttab, ttab, ltab, ltab],
        out_specs=[pl.BlockSpec((hw, TM), lambda i: (0, i)),
                   pl.BlockSpec((TM, hw), lambda i: (i, 0)),
                   pl.BlockSpec((TM // TK, vw, TK), lambda i: (i, 0, 0))],
        out_shape=[jax.ShapeDtypeStruct((hw, n), BF16), jax.ShapeDtypeStruct((n, hw), BF16),
                   jax.ShapeDtypeStruct((n // TK, vw, TK), BF16)],
        compiler_params=_cparams(("parallel",)),
        name="mla_proj",
    )(xb, w1, qg, kvg, wqt, wk, wvt, cqt, sqt, ck, sk)


def _chunk_of(pos):
    return lax.shift_right_logical(pos, 6)


def _diag_visible():
    k = lax.broadcasted_iota(I32, (TK, TQ), 0)
    q = lax.broadcasted_iota(I32, (TK, TQ), 1)
    return _chunk_of(k) <= _chunk_of(q)


def _online_step(h, s, vt, m_ref, l_ref, acc_ref):
    for c in range(s.shape[1] // LANES):
        cs = slice(c * LANES, (c + 1) * LANES)
        sc = s[:, cs]
        m_prev = m_ref[h, :, cs]
        m_new = jnp.maximum(m_prev, jnp.max(sc, axis=0, keepdims=True))
        alpha = jnp.exp2(m_prev - m_new)
        p = jnp.exp2(sc - m_new)
        l_ref[h, :, cs] = alpha * l_ref[h, :, cs] + jnp.sum(p, axis=0, keepdims=True)
        acc_ref[h, :, cs] = alpha * acc_ref[h, :, cs] + _dot(vt, p.astype(BF16))
        m_ref[h, :, cs] = m_new


def _init_state(m_ref, l_ref, acc_ref):
    m_ref[...] = jnp.full(m_ref.shape, NEG, F32)
    l_ref[...] = jnp.zeros(l_ref.shape, F32)
    acc_ref[...] = jnp.zeros(acc_ref.shape, F32)


def _write_heads(o_ref, l_ref, acc_ref, n_heads):
    for j in range(n_heads // 2):
        ot = jnp.concatenate([acc_ref[2 * j] / l_ref[2 * j], acc_ref[2 * j + 1] / l_ref[2 * j + 1]], axis=0)
        o_ref[:, j * LANES:(j + 1) * LANES] = ot.T.astype(o_ref.dtype)


def _mla_attn_kernel(qt_ref, k_ref, vt_ref, o_ref, m_ref, l_ref, acc_ref):
    i = pl.program_id(1)
    _init_state(m_ref, l_ref, acc_ref)
    diag = _diag_visible()

    def block(j, masked):
        rows = pl.ds(pl.multiple_of(j * TK, TK), TK)
        vt = vt_ref[j]
        ss = [_dot(k_ref[rows, h * LANES:(h + 1) * LANES], qt_ref[h * LANES:(h + 1) * LANES, :])
              for h in range(A_HEADS)]
        for h in range(A_HEADS):
            s = jnp.where(diag, ss[h], NEG) if masked else ss[h]
            _online_step(h, s, vt[h * A_VDIM:(h + 1) * A_VDIM], m_ref, l_ref, acc_ref)

    def body(j, c):
        block(j, False)
        return c

    lax.fori_loop(0, i, body, 0)
    block(i, True)
    _write_heads(o_ref, l_ref, acc_ref, A_HEADS)


def _attn_scratch(n_heads, dv):
    return [pltpu.VMEM((n_heads, 1, TQ), F32), pltpu.VMEM((n_heads, 1, TQ), F32),
            pltpu.VMEM((n_heads, dv, TQ), F32)]


def _mla_attn_call(qt, k, vt, batch, seq):
    nq = seq // TQ
    nkb = seq // TK
    hw = A_HEADS * LANES
    vw = A_HEADS * A_VDIM
    return pl.pallas_call(
        _mla_attn_kernel,
        grid=(batch, nq),
        in_specs=[pl.BlockSpec((hw, TQ), lambda b, i: (0, b * nq + i)),
                  pl.BlockSpec((seq, hw), lambda b, i: (b, 0)),
                  pl.BlockSpec((nkb, vw, TK), lambda b, i: (b, 0, 0))],
        out_specs=pl.BlockSpec((TQ, vw), lambda b, i: (b * nq + i, 0)),
        out_shape=jax.ShapeDtypeStruct((batch * seq, vw), BF16),
        scratch_shapes=_attn_scratch(A_HEADS, A_VDIM),
        compiler_params=_cparams(("parallel", "arbitrary")),
        name="mla_attn",
    )(qt, k, vt)


def _dsa_attn_kernel(n_sel, qw_ref, k_ref, vt_ref, qit_ref, ki_ref, wt_ref, o_ref,
                     key_ref, part_ref, bias_ref, j_ref, m_ref, l_ref, acc_ref):
    i = pl.program_id(1)
    nblk = i + 1
    diag = _diag_visible()
    kpos = lax.broadcasted_iota(I32, (TK, TQ), 0)
    w = wt_ref[...]

    def score_block(jb, c):
        ki = ki_ref[pl.ds(pl.multiple_of(jb * TK, TK), TK), :]
        sc = jnp.zeros((TK, TQ), F32)
        for h in range(IDX_HEADS):
            rel = jnp.maximum(_dot(ki, qit_ref[h * LANES:(h + 1) * LANES, :]), 0.0)
            sc = sc + w[h:h + 1, :] * rel
        sc = jnp.where(jnp.logical_or(jb < i, diag), sc, NEG)
        sc = jnp.where(jnp.abs(sc) < F32_TINY, 0.0, sc)
        bits = lax.bitcast_convert_type(sc, I32)
        key_ref[jb] = jnp.where(bits < 0, bits ^ 0x7FFFFFFF, bits)
        part_ref[jb] = lax.bitcast_convert_type(bits & -65536, F32).astype(BF16)
        return c

    lax.fori_loop(0, nblk, score_block, 0)

    one16 = jnp.ones((TK, TQ), BF16)
    zero16 = jnp.zeros((TK, TQ), BF16)
    rows16 = 2 * SUBLANES

    def count16(pred):
        def body(jb, acc):
            ind = jnp.where(pred(part_ref[jb]), one16, zero16).reshape(TK // rows16, rows16, TQ)
            parts = [ind[r] for r in range(TK // rows16)]
            while len(parts) > 1:
                parts = [parts[a] + parts[a + 1] for a in range(0, len(parts), 2)]
            return acc + parts[0]
        acc = lax.fori_loop(0, nblk, body, jnp.zeros((rows16, TQ), BF16))
        return jnp.sum(acc.astype(F32), axis=0, keepdims=True).astype(I32)

    def greedy(n_bits, start, to_float, want):
        def step(s, v):
            cand = v | lax.shift_left(jnp.int32(1), n_bits - 1 - s)
            cf = to_float(cand)
            c = count16(lambda x: x >= cf)
            return jnp.where(c >= want, cand, v)
        return lax.fori_loop(0, n_bits, step, start)

    def upper_float(kh):
        p = jnp.where(kh >= 0, kh, kh ^ 0x7FFF)
        p = jnp.where(jnp.logical_and(p > 0, p < 0x80), 0x80, p)
        return lax.bitcast_convert_type(lax.shift_left(p, 16), F32).astype(BF16)

    small_float = lambda v: v.astype(F32).astype(BF16)

    c0 = count16(lambda x: x >= jnp.zeros((1, TQ), BF16))
    th = greedy(15, jnp.where(c0 >= n_sel, 0, -32768).astype(I32), upper_float, n_sel)
    thf = upper_float(th)
    need_m = n_sel - count16(lambda x: x > thf)

    def mid_block(jb, c):
        k = key_ref[jb]
        band = lax.shift_right_arithmetic(k, 16) == th
        mid = (lax.shift_right_logical(k, 8) & 0xFF).astype(F32)
        part_ref[jb] = jnp.where(band, mid, -1.0).astype(BF16)
        return c

    lax.fori_loop(0, nblk, mid_block, 0)
    tm = greedy(8, jnp.zeros((1, TQ), I32), small_float, need_m)
    tmf = small_float(tm)
    need_l = need_m - count16(lambda x: x > tmf)

    t24 = lax.shift_left(th, 8) | tm

    def low_block(jb, c):
        k = key_ref[jb]
        band = lax.shift_right_arithmetic(k, 8) == t24
        part_ref[jb] = jnp.where(band, (k & 0xFF).astype(F32), -1.0).astype(BF16)
        return c

    lax.fori_loop(0, nblk, low_block, 0)
    tl = greedy(8, jnp.zeros((1, TQ), I32), small_float, need_l)
    tlf = small_float(tl)
    t = lax.shift_left(t24, 8) | tl
    need = need_l - count16(lambda x: x > tlf)
    c_eq = count16(lambda x: x == tlf)

    j_ref[...] = jnp.full(j_ref.shape, 2 ** 30, I32)

    @pl.when(jnp.max(c_eq - need) > 0)
    def _():
        def scan(jb, carry):
            cum, blk, want = carry
            ind = jnp.where(part_ref[jb] == tlf, one16, zero16).reshape(TK // rows16, rows16, TQ)
            parts = [ind[r] for r in range(TK // rows16)]
            while len(parts) > 1:
                parts = [parts[a] + parts[a + 1] for a in range(0, len(parts), 2)]
            new = cum + jnp.sum(parts[0].astype(F32), axis=0, keepdims=True).astype(I32)
            hit = jnp.logical_and(cum < need, new >= need)
            return new, jnp.where(hit, jb, blk), jnp.where(hit, need - cum, want)

        zero_q = jnp.zeros((1, TQ), I32)
        _, blk, want = lax.fori_loop(0, nblk, scan, (zero_q, zero_q, zero_q + 1))

        row16 = kpos.astype(F32).astype(BF16)
        far16 = jnp.full((TK, TQ), 2.0 * TK, BF16)

        def mark(jb, c):
            target = jnp.where(blk == jb, tl, -2).astype(F32).astype(BF16)
            part_ref[jb] = jnp.where(part_ref[jb] == target, row16, far16)
            return c

        lax.fori_loop(0, nblk, mark, 0)

        def row_step(s, v):
            cand = v | lax.shift_left(jnp.int32(1), 7 - s)
            cf = small_float(cand)
            return jnp.where(count16(lambda x: x < cf) < want, cand, v)

        j_ref[...] = blk * TK + lax.fori_loop(0, 8, row_step, zero_q)

    jv = j_ref[...]

    def bias_block(jb, c):
        k = key_ref[jb]
        g = jb * TK + kpos
        sel = jnp.logical_or(k > t, jnp.logical_and(k == t, g <= jv))
        vis = jnp.logical_or(jb < i, diag)
        bias_ref[jb] = jnp.where(jnp.logical_and(sel, vis), 0.0, NEG)
        return c

    lax.fori_loop(0, nblk, bias_block, 0)

    _init_state(m_ref, l_ref, acc_ref)

    def attn_block(jb, c):
        kb = k_ref[pl.ds(pl.multiple_of(jb * TK, TK), TK), :]
        vt = vt_ref[jb]
        bias = bias_ref[jb]
        s = _dot(kb, qw_ref[0])
        rep = B_HEADS // B_KV_HEADS
        for cidx in range(B_HEADS * TQ // LANES):
            cs = slice(cidx * LANES, (cidx + 1) * LANES)
            g = cidx * LANES // (rep * TQ)
            bc = (cidx * LANES) % TQ
            sc = s[:, cs] + bias[:, bc:bc + LANES]
            m_prev = m_ref[:, cs]
            m_new = jnp.maximum(m_prev, jnp.max(sc, axis=0, keepdims=True))
            alpha = jnp.exp2(m_prev - m_new)
            p = jnp.exp2(sc - m_new)
            l_ref[:, cs] = alpha * l_ref[:, cs] + jnp.sum(p, axis=0, keepdims=True)
            acc_ref[:, cs] = alpha * acc_ref[:, cs] + _dot(vt[g * B_HDIM:(g + 1) * B_HDIM], p.astype(BF16))
            m_ref[:, cs] = m_new
        return c

    lax.fori_loop(0, nblk, attn_block, 0)
    _write_wide(o_ref, acc_ref[...] / l_ref[...], B_HEADS)


def _gqa_pv(vts, pb, n_groups, hdim):
    gw = pb.shape[1] // n_groups
    outs = []
    for g in range(n_groups):
        o = None
        k0 = 0
        for vt in vts:
            kc = vt.shape[1]
            part = _dot(vt[g * hdim:(g + 1) * hdim], pb[k0:k0 + kc, g * gw:(g + 1) * gw])
            o = part if o is None else o + part
            k0 += kc
        outs.append(o)
    return jnp.concatenate(outs, axis=1)


def _write_wide(o_ref, ow, n_heads):
    for j in range(n_heads // 2):
        ot = jnp.concatenate([ow[:, 2 * j * TQ:(2 * j + 1) * TQ], ow[:, (2 * j + 1) * TQ:(2 * j + 2) * TQ]], axis=0)
        o_ref[:, j * LANES:(j + 1) * LANES] = ot.T.astype(o_ref.dtype)


def _wide_scratch(n_heads, dv):
    return [pltpu.VMEM((1, n_heads * TQ), F32), pltpu.VMEM((1, n_heads * TQ), F32),
            pltpu.VMEM((dv, n_heads * TQ), F32)]


def _dsa_attn_call(qw, k, vt, qit, ki, wt, batch, seq):
    nq = seq // TQ
    nkb = seq // TK
    n_sel = min(TOPK_MAX, seq // 4)
    qcol = lambda rows: pl.BlockSpec((rows, TQ), lambda b, i: (0, b * nq + i))
    brow = lambda wdt: pl.BlockSpec((seq, wdt), lambda b, i: (b, 0))
    ow = B_HEADS * B_HDIM
    return pl.pallas_call(
        functools.partial(_dsa_attn_kernel, n_sel),
        grid=(batch, nq),
        in_specs=[pl.BlockSpec((1,) + qw.shape[1:], lambda b, i: (b * nq + i, 0, 0)), brow(k.shape[1]),
                  pl.BlockSpec((nkb,) + vt.shape[1:], lambda b, i: (b, 0, 0)),
                  qcol(qit.shape[0]), brow(ki.shape[1]), qcol(wt.shape[0])],
        out_specs=pl.BlockSpec((TQ, ow), lambda b, i: (b * nq + i, 0)),
        out_shape=jax.ShapeDtypeStruct((batch * seq, ow), BF16),
        scratch_shapes=[pltpu.VMEM((nkb, TK, TQ), I32), pltpu.VMEM((nkb, TK, TQ), BF16),
                        pltpu.VMEM((nkb, TK, TQ), F32),
                        pltpu.VMEM((1, TQ), I32)] + _wide_scratch(B_HEADS, B_HDIM),
        compiler_params=_cparams(("parallel", "arbitrary")),
        name="dsa_attn",
    )(qw, k, vt, qit, ki, wt)


SWA_SLAB = TQ + WIN_CHUNKS * CHUNK


def _swa_attn_kernel(sink_ref, qw_ref, k_ref, vt_ref, o_ref):
    i = pl.program_id(1)
    s0 = pl.multiple_of(jnp.maximum(i * TQ - SWA_KB, 0), SWA_KB)
    kslab = k_ref[pl.ds(s0, SWA_SLAB), :]
    kc = _chunk_of(s0 + lax.broadcasted_iota(I32, (SWA_SLAB, TQ), 0))
    qc = _chunk_of(i * TQ + lax.broadcasted_iota(I32, (SWA_SLAB, TQ), 1))
    valid = jnp.logical_and(kc <= qc, kc >= qc - WIN_CHUNKS)
    b0 = lax.div(s0, SWA_KB)
    vts = [vt_ref[b0 + c] for c in range(SWA_SLAB // SWA_KB)]
    s = _dot(kslab, qw_ref[0])
    s = jnp.concatenate([jnp.where(valid, s[:, h * TQ:(h + 1) * TQ], NEG) for h in range(C_HEADS)], axis=1)
    sink = jnp.concatenate([jnp.full((1, TQ), sink_ref[h] * LOG2E, F32) for h in range(C_HEADS)], axis=1)
    m = jnp.maximum(jnp.max(s, axis=0, keepdims=True), sink)
    p = jnp.exp2(s - m)
    l = jnp.sum(p, axis=0, keepdims=True) + jnp.exp2(sink - m)
    _write_wide(o_ref, _gqa_pv(vts, p.astype(BF16), C_KV_HEADS, C_HDIM) / l, C_HEADS)


def _swa_attn_call(sinks, qw, k, vt, batch, seq):
    nq = seq // TQ
    nkb = seq // SWA_KB
    ow = C_HEADS * C_HDIM
    return pl.pallas_call(
        _swa_attn_kernel,
        grid=(batch, nq),
        in_specs=[pl.BlockSpec(memory_space=pltpu.SMEM),
                  pl.BlockSpec((1,) + qw.shape[1:], lambda b, i: (b * nq + i, 0, 0)),
                  pl.BlockSpec((seq, k.shape[1]), lambda b, i: (b, 0)),
                  pl.BlockSpec((nkb,) + vt.shape[1:], lambda b, i: (b, 0, 0))],
        out_specs=pl.BlockSpec((TQ, ow), lambda b, i: (b * nq + i, 0)),
        out_shape=jax.ShapeDtypeStruct((batch * seq, ow), BF16),
        compiler_params=_cparams(("parallel", "arbitrary")),
        name="swa_attn",
    )(sinks, qw, k, vt)


def _merge_kernel(oa_ref, ob_ref, oc_ref, ga_ref, gb_ref, gc_ref, x_ref, wa_ref, wb_ref, wc_ref, wo_ref,
                  lg_ref, lb_ref, wr_ref, rb_ref, of_ref, idx_ref, wgt_ref, cnt_ref, mask_ref, carry_ref):
    merged = (ga_ref[...].astype(F32) * _dot(oa_ref[...], wa_ref[...])
              + gb_ref[...].astype(F32) * _dot(ob_ref[...], wb_ref[...])
              + gc_ref[...].astype(F32) * _dot(oc_ref[...], wc_ref[...]))
    mix = _dot(merged.astype(BF16), wo_ref[...])
    y = _layer_norm(ALPHA * x_ref[...] + mix, lg_ref[...], lb_ref[...])
    of_ref[...] = y
    _route(y, wr_ref, rb_ref, idx_ref, wgt_ref, cnt_ref, mask_ref, carry_ref)


def _merge_call(oa, ob, oc, ga, gb, gc, x, wa, wb, wc, wo, lg, lb, wr_t, rb):
    n, d = x.shape
    row = lambda wdt: pl.BlockSpec((TM, wdt), lambda i: (i, 0))
    full = lambda a: pl.BlockSpec(a.shape, lambda i: (0, 0))
    col = pl.BlockSpec((SUBLANES, TM), lambda i: (0, i))
    return pl.pallas_call(
        _merge_kernel,
        grid=(n // TM,),
        in_specs=[row(oa.shape[1]), row(ob.shape[1]), row(oc.shape[1]), row(d), row(d), row(d), row(d),
                  full(wa), full(wb), full(wc), full(wo), full(lg), full(lb), full(wr_t), full(rb)],
        out_specs=[row(d), col, col, pl.BlockSpec((N_EXPERTS, LANES), lambda i: (0, 0))],
        out_shape=[jax.ShapeDtypeStruct((n, d), F32), jax.ShapeDtypeStruct((SUBLANES, n), I32),
                   jax.ShapeDtypeStruct((SUBLANES, n), F32), jax.ShapeDtypeStruct((N_EXPERTS, LANES), F32)],
        scratch_shapes=[pltpu.VMEM((N_EXPERTS, TM), F32), pltpu.VMEM((N_EXPERTS, 1), F32)],
        compiler_params=_cparams(("arbitrary",)),
        name="merge",
    )(oa, ob, oc, ga, gb, gc, x, wa, wb, wc, wo, lg, lb, wr_t, rb)


def _route(x, wr_ref, rb_ref, idx_ref, wgt_ref, cnt_ref, mask_ref, carry_ref):
    @pl.when(pl.program_id(0) == 0)
    def _():
        carry_ref[...] = jnp.zeros(carry_ref.shape, F32)

    logits = lax.dot_general(wr_ref[...], x, (((1,), (1,)), ((), ())),
                             precision=lax.Precision.HIGHEST, preferred_element_type=F32)
    scores = jax.nn.sigmoid(logits)
    biased = scores + rb_ref[...]
    sc = [scores[e:e + 1, :] for e in range(N_EXPERTS)]
    bi = [biased[e:e + 1, :] for e in range(N_EXPERTS)]
    epg = EXPERTS_PER_GROUP
    gs = []
    for g in range(N_GROUPS):
        b = bi[g * epg:(g + 1) * epg]
        best = None
        for a in range(epg):
            for c in range(a + 1, epg):
                pair = b[a] + b[c]
                best = pair if best is None else jnp.maximum(best, pair)
        gs.append(best)
    gsel = []
    for g in range(N_GROUPS):
        ok = None
        for o in range(N_GROUPS):
            if o == g:
                continue
            cond = gs[g] > gs[o] if o < g else gs[g] >= gs[o]
            ok = cond if ok is None else jnp.logical_and(ok, cond)
        gsel.append(ok)
    chosen = []
    for e in range(N_EXPERTS):
        g = e // epg
        rank = jnp.zeros_like(bi[e], dtype=I32)
        for o in range(g * epg, (g + 1) * epg):
            if o == e:
                continue
            ahead = bi[o] >= bi[e] if o < e else bi[o] > bi[e]
            rank = rank + ahead.astype(I32)
        chosen.append(jnp.logical_and(gsel[g], rank < TOP_K))
    picked = [jnp.where(chosen[e], sc[e], 0.0) for e in range(N_EXPERTS)]
    total = picked[0]
    for e in range(1, N_EXPERTS):
        total = total + picked[e]
    for e in range(N_EXPERTS):
        mask_ref[e:e + 1, :] = chosen[e].astype(F32)
    mask = mask_ref[...]
    tm = mask.shape[1]
    upper = (lax.broadcasted_iota(I32, (tm, tm), 0) <= lax.broadcasted_iota(I32, (tm, tm), 1)).astype(BF16)
    pos = carry_ref[...] + _dot(mask.astype(BF16), upper) - 1.0
    carry_ref[...] += jnp.sum(mask, axis=1, keepdims=True)
    cnt_ref[...] = jnp.broadcast_to(carry_ref[...], cnt_ref.shape)
    zero_i = jnp.zeros_like(bi[0], dtype=I32)
    zero_f = jnp.zeros_like(bi[0])
    seen = None
    e_lo, e_hi, p_lo, p_hi, w_lo, w_hi = zero_i, zero_i, zero_f, zero_f, zero_f, zero_f
    for e in range(N_EXPERTS):
        is_lo = chosen[e] if seen is None else jnp.logical_and(chosen[e], jnp.logical_not(seen))
        is_hi = jnp.zeros_like(chosen[e]) if seen is None else jnp.logical_and(chosen[e], seen)
        seen = chosen[e] if seen is None else jnp.logical_or(seen, chosen[e])
        pe = pos[e:e + 1, :]
        we = picked[e] / total
        e_lo = jnp.where(is_lo, e, e_lo)
        e_hi = jnp.where(is_hi, e, e_hi)
        p_lo = jnp.where(is_lo, pe, p_lo)
        p_hi = jnp.where(is_hi, pe, p_hi)
        w_lo = jnp.where(is_lo, we, w_lo)
        w_hi = jnp.where(is_hi, we, w_hi)
    idx_ref[...] = jnp.zeros(idx_ref.shape, I32)
    idx_ref[0:1, :] = e_lo
    idx_ref[1:2, :] = e_hi
    idx_ref[2:3, :] = p_lo.astype(I32)
    idx_ref[3:4, :] = p_hi.astype(I32)
    wgt_ref[...] = jnp.zeros(wgt_ref.shape, F32)
    wgt_ref[0:1, :] = w_lo
    wgt_ref[1:2, :] = w_hi


def _row_copy(src, src_row, dst, dst_row, sem):
    return pltpu.make_async_copy(src.at[pl.ds(src_row, 1)], dst.at[pl.ds(dst_row, 1)], sem)


def _dispatch_kernel(dlo_ref, dhi_ref, pad_ref, x_ref, xs_hbm, zero_ref, sem):
    i = pl.program_id(0)

    @pl.when(i == 0)
    def _():
        zero_ref[...] = jnp.zeros(zero_ref.shape, F32)
        for j in range(2 * N_EXPERTS):
            @pl.when(pad_ref[j] >= 0)
            def _():
                first = pl.multiple_of(pad_ref[j], TM_MOE)
                cp = pltpu.make_async_copy(zero_ref, xs_hbm.at[pl.ds(first, TM_MOE)], sem.at[2])
                cp.start()
                cp.wait()

    base = i * TM

    for r in range(TM):
        t = base + r
        _row_copy(x_ref, r, xs_hbm, dlo_ref[t], sem.at[0]).start(priority=r % 2)
        _row_copy(x_ref, r, xs_hbm, dhi_ref[t], sem.at[1]).start(priority=(r + 1) % 2)
    pltpu.make_async_copy(x_ref, xs_hbm.at[pl.ds(0, TM)], sem.at[0]).wait()
    pltpu.make_async_copy(x_ref, xs_hbm.at[pl.ds(0, TM)], sem.at[1]).wait()


def _dispatch_call(dlo, dhi, pad_start, xf, n_rows):
    n, d = xf.shape
    return pl.pallas_call(
        _dispatch_kernel,
        grid_spec=pltpu.PrefetchScalarGridSpec(
            num_scalar_prefetch=3,
            grid=(n // TM,),
            in_specs=[pl.BlockSpec((TM, d), lambda i, a, b, c: (i, 0))],
            out_specs=pl.BlockSpec(memory_space=pl.ANY),
            scratch_shapes=[pltpu.VMEM((TM_MOE, d), F32), pltpu.SemaphoreType.DMA((3,))]),
        out_shape=jax.ShapeDtypeStruct((n_rows, d), F32),
        compiler_params=_cparams(("arbitrary",)),
        name="moe_dispatch",
    )(dlo, dhi, pad_start, xf)


def _experts_kernel(te_ref, nt_ref, first_ref, nxt_ref, slot_ref, xs_ref, wg_hbm, wu_hbm, wd_hbm, ys_ref,
                    wg32_ref, wu32_ref, wd32_ref, wgb_ref, wub_ref, wdb_ref, sem):
    i = pl.program_id(0)
    used = i < nt_ref[0]

    def weight_copies(e, s):
        return (pltpu.make_async_copy(wg_hbm.at[e], wg32_ref.at[s], sem.at[0, s]),
                pltpu.make_async_copy(wu_hbm.at[e], wu32_ref.at[s], sem.at[1, s]),
                pltpu.make_async_copy(wd_hbm.at[e], wd32_ref.at[s], sem.at[2, s]))

    @pl.when(i == 0)
    def _():
        for cp in weight_copies(te_ref[0], slot_ref[0]):
            cp.start()

    @pl.when(first_ref[i] == 1)
    def _():
        s = slot_ref[i]
        for cp in weight_copies(te_ref[i], s):
            cp.wait()

        @pl.when(nxt_ref[i] >= 0)
        def _():
            for cp in weight_copies(nxt_ref[i], 1 - s):
                cp.start()

        wgb_ref[...] = wg32_ref[s].astype(BF16)
        wub_ref[...] = wu32_ref[s].astype(BF16)
        wdb_ref[...] = wd32_ref[s].astype(BF16)

    @pl.when(used)
    def _():
        x = xs_ref[...].astype(BF16)
        a = _dot(x, wgb_ref[...])
        u = _dot(x, wub_ref[...])
        h = a * jax.nn.sigmoid(a) * u
        ys_ref[...] = _dot(h.astype(BF16), wdb_ref[...])

    @pl.when(jnp.logical_not(used))
    def _():
        ys_ref[...] = jnp.zeros(ys_ref.shape, F32)


def _experts_call(tile_expert, n_tiles, first, nxt, slot, xs, wg, wu, wd):
    n_rows, d = xs.shape
    tile = lambda i, te, nt, fi, nx, sl: (jnp.maximum(jnp.minimum(i, nt[0] - 1), 0), 0)
    hbm = pl.BlockSpec(memory_space=pl.ANY)
    return pl.pallas_call(
        _experts_kernel,
        grid_spec=pltpu.PrefetchScalarGridSpec(
            num_scalar_prefetch=5,
            grid=(n_rows // TM_MOE,),
            in_specs=[pl.BlockSpec((TM_MOE, d), tile), hbm, hbm, hbm],
            out_specs=pl.BlockSpec((TM_MOE, d), lambda i, te, nt, fi, nx, sl: (i, 0)),
            scratch_shapes=[pltpu.VMEM((2, d, D_EXPERT), F32), pltpu.VMEM((2, d, D_EXPERT), F32),
                            pltpu.VMEM((2, D_EXPERT, d), F32),
                            pltpu.VMEM((d, D_EXPERT), BF16), pltpu.VMEM((d, D_EXPERT), BF16),
                            pltpu.VMEM((D_EXPERT, d), BF16), pltpu.SemaphoreType.DMA((3, 2))]),
        out_shape=jax.ShapeDtypeStruct((n_rows, d), F32),
        compiler_params=_cparams(("arbitrary",)),
        name="moe_experts",
    )(tile_expert, n_tiles, first, nxt, slot, xs, wg, wu, wd)


def _combine_kernel(dlo_ref, dhi_ref, ys_hbm, xf_ref, w_ref, lg_ref, lb_ref, of_ref, ob_ref, a_ref, b_ref, sem):
    i = pl.program_id(0)
    n = pl.num_programs(0)

    def issue(tile, slot):
        base = tile * TM_CMB

        for r in range(TM_CMB):
            t = base + r
            _row_copy(ys_hbm, dlo_ref[t], a_ref.at[slot], r, sem.at[0, slot]).start(priority=r % 2)
            _row_copy(ys_hbm, dhi_ref[t], b_ref.at[slot], r, sem.at[1, slot]).start(priority=(r + 1) % 2)

    @pl.when(i == 0)
    def _():
        issue(0, 0)

    @pl.when(i + 1 < n)
    def _():
        issue(i + 1, (i + 1) % 2)

    slot = i % 2
    pltpu.make_async_copy(ys_hbm.at[pl.ds(0, TM_CMB)], a_ref.at[slot], sem.at[0, slot]).wait()
    pltpu.make_async_copy(ys_hbm.at[pl.ds(0, TM_CMB)], b_ref.at[slot], sem.at[1, slot]).wait()
    w = w_ref[...]
    ffn = w[:, 0:1] * a_ref[slot] + w[:, 1:2] * b_ref[slot]
    y = _layer_norm(ALPHA * xf_ref[...] + ffn, lg_ref[...], lb_ref[...])
    of_ref[...] = y
    ob_ref[...] = y.astype(BF16)


def _combine_call(dlo, dhi, ys, xf, w2, lg, lb):
    n, d = xf.shape
    row = lambda wdt: pl.BlockSpec((TM_CMB, wdt), lambda i, a, b: (i, 0))
    vec = pl.BlockSpec((1, d), lambda i, a, b: (0, 0))
    return pl.pallas_call(
        _combine_kernel,
        grid_spec=pltpu.PrefetchScalarGridSpec(
            num_scalar_prefetch=2,
            grid=(n // TM_CMB,),
            in_specs=[pl.BlockSpec(memory_space=pl.ANY), row(d), row(2), vec, vec],
            out_specs=[row(d), row(d)],
            scratch_shapes=[pltpu.VMEM((2, TM_CMB, d), F32), pltpu.VMEM((2, TM_CMB, d), F32),
                            pltpu.SemaphoreType.DMA((2, 2))]),
        out_shape=[jax.ShapeDtypeStruct((n, d), F32), jax.ShapeDtypeStruct((n, d), BF16)],
        compiler_params=_cparams(("arbitrary",)),
        name="moe_combine",
    )(dlo, dhi, ys, xf, w2, lg, lb)


def _moe(xf, idx, wgt, cnt, wg, wu, wd, expert_base, lg, lb):
    n, d = xf.shape
    counts = cnt[:, 0].astype(I32)
    padded = (counts + TM_MOE - 1) // TM_MOE * TM_MOE
    ends = jnp.cumsum(padded)
    starts = ends - padded
    max_tiles = (TOP_K * n) // TM_MOE + N_EXPERTS
    n_tiles = (ends[-1] // TM_MOE).astype(I32).reshape(1)
    tile_ids = jnp.arange(max_tiles, dtype=I32)
    tile_expert = jnp.minimum(jnp.sum((tile_ids[:, None] >= (ends // TM_MOE)[None, :]).astype(I32), axis=1),
                              N_EXPERTS - 1)
    onehot = lambda e: (e[None, :] == jnp.arange(N_EXPERTS, dtype=I32)[:, None])
    dlo = jnp.sum(jnp.where(onehot(idx[0]), starts[:, None], 0), axis=0) + idx[2]
    dhi = jnp.sum(jnp.where(onehot(idx[1]), starts[:, None], 0), axis=0) + idx[3]
    tail = n_tiles[0] + jnp.arange(N_EXPERTS, dtype=I32)
    pad_start = jnp.concatenate([jnp.where(counts % TM_MOE != 0, ends - TM_MOE, -1),
                                 jnp.where(tail < max_tiles, tail * TM_MOE, -1)]).astype(I32)
    xs = _dispatch_call(dlo, dhi, pad_start, xf, max_tiles * TM_MOE)
    experts = jnp.arange(N_EXPERTS, dtype=I32)
    live = padded > 0
    run_rank = jnp.cumsum(live.astype(I32)) - live.astype(I32)
    later = jnp.logical_and(experts[None, :] > experts[:, None], live[None, :])
    next_live = jnp.min(jnp.where(later, experts[None, :], N_EXPERTS), axis=1)
    next_e = jnp.where(next_live < N_EXPERTS, next_live + expert_base, -1)
    per_tile = lambda tbl: jnp.sum(jnp.where(tile_expert[:, None] == experts[None, :], tbl[None, :], 0), axis=1)
    used_tile = tile_ids < n_tiles[0]
    first = jnp.logical_and(used_tile, tile_ids == per_tile(starts // TM_MOE)).astype(I32)
    nxt = jnp.where(used_tile, per_tile(next_e), -1).astype(I32)
    slot = (per_tile(run_rank) % 2).astype(I32)
    ys = _experts_call(tile_expert + expert_base, n_tiles, first, nxt, slot, xs, wg, wu, wd)
    return _combine_call(dlo, dhi, ys, xf, wgt[:2].T, lg, lb)


def _rope_tables(seq):
    pos = jnp.arange(seq, dtype=F32)[:, None]
    lane = np.arange(LANES)

    def angles(half):
        inv = ROPE_THETA ** (-jnp.arange(half, dtype=F32) / half)
        return pos * inv[None, :]

    def lane_table(half, lo, hi):
        ang = angles(half)
        idx = (lane - lo) % half
        active = (lane >= lo) & (lane < hi)
        first = ((lane - lo) % (2 * half)) < half
        cos = jnp.where(active[None, :], jnp.cos(ang)[:, idx], 1.0)
        sin = jnp.where(active[None, :], jnp.sin(ang)[:, idx] * np.where(first, -1.0, 1.0)[None, :], 0.0)
        return cos.astype(F32), sin.astype(F32)

    def feat_table(half):
        ang = angles(half)
        return jnp.cos(ang).T.astype(F32), jnp.sin(ang).T.astype(F32)

    c64, s64 = lane_table(B_HDIM // 2, 0, LANES)
    c64t, s64t = feat_table(B_HDIM // 2)
    ck, sk = lane_table(A_ROPE // 2, 0, A_ROPE)
    cqt, sqt = feat_table(A_ROPE // 2)
    return (c64, s64, c64t, s64t), (cqt, sqt, ck, sk)


def _layer_weights(l, w_in, a_w_uq, a_w_ukv):
    d = D_MODEL
    pts = [int(p) for p in np.cumsum(SPLIT_SIZES)[:-1]]
    (a_q, a_kv, a_kpe, b_q, b_k, b_v, b_qi, b_ki, b_wi, c_q, c_k, c_v, gates) = jnp.split(w_in[l], pts, axis=1)
    bf = lambda a: a.astype(BF16)
    zeros = lambda n: jnp.zeros((d, n), BF16)
    w_mla = jnp.concatenate([bf(a_q), bf(a_kv), bf(a_kpe), zeros(LANES - A_ROPE)], axis=1)
    dsa_w = jnp.concatenate([bf(b_k), bf(b_ki), zeros(LANES - IDX_DIM)], axis=1)
    qi_t = jnp.concatenate([bf(b_qi).reshape(d, IDX_HEADS, IDX_DIM),
                            jnp.zeros((d, IDX_HEADS, LANES - IDX_DIM), BF16)], axis=2).reshape(d, -1).T
    wi_t = jnp.concatenate([bf(b_wi * (IDX_HEADS * IDX_DIM) ** -0.5), zeros(WI_ROWS - IDX_HEADS)], axis=1).T
    dsa_wt = jnp.concatenate([bf(b_q * (B_HDIM ** -0.5 * LOG2E)).T, bf(b_v).T, qi_t, wi_t], axis=0)
    swa_wt = jnp.concatenate([bf(c_q * (C_HDIM ** -0.5 * LOG2E)).T, bf(c_v).T], axis=0)
    uq = bf(a_w_uq[l].reshape(A_Q_RANK, A_HEADS, A_NOPE + A_ROPE) * ((A_NOPE + A_ROPE) ** -0.5 * LOG2E))
    wq = jnp.concatenate([uq, jnp.zeros((A_Q_RANK, A_HEADS, LANES - A_NOPE - A_ROPE), BF16)], axis=2)
    ukv = bf(a_w_ukv[l]).reshape(A_KV_RANK, A_HEADS, A_NOPE + A_VDIM)
    wk = jnp.concatenate([ukv[:, :, :A_NOPE], jnp.zeros((A_KV_RANK, A_HEADS, LANES - A_NOPE), BF16)], axis=2)
    wv = ukv[:, :, A_NOPE:]
    return dict(w_mla=w_mla, dsa_w=dsa_w, dsa_wt=dsa_wt, swa_w=bf(c_k), swa_wt=swa_wt,
                w_gate=bf(gates), wqt=wq.reshape(A_Q_RANK, -1).T, wk=wk.reshape(A_KV_RANK, -1),
                wvt=wv.reshape(A_KV_RANK, -1).T)


QW = B_HEADS * B_HDIM
DSA_TOK = ((0, LANES, True), (LANES, LANES, True))
DSA_FEAT = ((0, QW, True, "wide"), (QW, LANES, False, TK), (QW + LANES, IDX_HEADS * LANES, True, 0),
            (QW + LANES + IDX_HEADS * LANES, WI_ROWS, False, 0))
SWA_TOK = ((0, LANES, True),)
SWA_FEAT = ((0, QW, True, "wide"), (QW, LANES, False, SWA_KB))


def kernel(x, ln_in_g, ln_in_b, w_in, a_q_ln_g, a_kv_ln_g, a_w_uq, a_w_ukv, c_sinks, w_br_a, w_br_b, w_br_c,
           w_out, ln1_g, ln1_b, w_router, router_bias, w_exp_gate, w_exp_up, w_exp_down, ln2_g, ln2_b):
    batch, seq, d = x.shape
    assert d == D_MODEL and seq % TM == 0 and seq % TQ == 0 and TQ == TK
    n = batch * seq
    tabs64, mla_tabs = _rope_tables(seq)
    xf, xb = _ln_call(x.reshape(n, d), ln_in_g, ln_in_b)
    wr_t = w_router.T
    rb = router_bias.reshape(N_EXPERTS, 1)
    wg_all = w_exp_gate.reshape(DEPTH * N_EXPERTS, d, D_EXPERT)
    wu_all = w_exp_up.reshape(DEPTH * N_EXPERTS, d, D_EXPERT)
    wd_all = w_exp_down.reshape(DEPTH * N_EXPERTS, D_EXPERT, d)
    bf = lambda a: a.astype(BF16)
    vec = lambda a: a.reshape(1, -1)
    for l in range(DEPTH):
        w = _layer_weights(l, w_in, a_w_uq, a_w_ukv)
        qa, ka, va = _mla_proj_call(xb, w["w_mla"], vec(a_q_ln_g[l]), vec(a_kv_ln_g[l]), w["wqt"], w["wk"],
                                    w["wvt"], mla_tabs, seq)
        o_a = _mla_attn_call(qa, ka, va, batch, seq)
        bk, bki, bq, bv, bqi, bwi = _qkv_proj_call(xb, w["dsa_w"], w["dsa_wt"], tabs64, DSA_TOK, (BF16, BF16),
                                                   DSA_FEAT, (BF16, BF16, BF16, F32), "dsa_proj", seq)
        o_b = _dsa_attn_call(bq, bk, bv, bqi, bki, bwi, batch, seq)
        ck_, cq_, cv_ = _qkv_proj_call(xb, w["swa_w"], w["swa_wt"], tabs64, SWA_TOK, (BF16,),
                                       SWA_FEAT, (BF16, BF16), "swa_proj", seq)
        o_c = _swa_attn_call(c_sinks[l], cq_, ck_, cv_, batch, seq)
        g0, g1, g2 = _gate_proj_call(xb, w["w_gate"])
        xf, idx, wgt, cnt = _merge_call(o_a, o_b, o_c, g0, g1, g2, xf, bf(w_br_a[l]), bf(w_br_b[l]),
                                        bf(w_br_c[l]), bf(w_out[l]), vec(ln1_g[l]), vec(ln1_b[l]), wr_t, rb)
        xf, xb = _moe(xf, idx, wgt, cnt, wg_all, wu_all, wd_all, l * N_EXPERTS, vec(ln2_g[l]), vec(ln2_b[l]))
    return xf.reshape(batch, seq, d)
```
